```python
import math
import jax, jax.numpy as jnp
from jax import lax
import numpy as np


D_MODEL = 2048
BATCH = 8
SEQ = 2048
DEPTH = 1

A_HEADS = 8
A_HEAD_DIM = 128
A_WIDTH = A_HEADS * A_HEAD_DIM
MOBA_BLOCK = 256
MOBA_TOPK = 3
MOBA_Q_CHUNK = 16
REL_BUCKETS = 32
REL_MAX_DIST = 128
B_HEADS = 8
B_KEY_DIM = 128
B_VAL_DIM = 128
B_KW = B_HEADS * B_KEY_DIM
B_VW = B_HEADS * B_VAL_DIM
CONV_WIDTH = 4
DELTA_CHUNK = 64
PEER_HEADS = 8
PEER_NKEYS = 128
PEER_EXPERTS = PEER_NKEYS * PEER_NKEYS
PEER_QDIM = 256
PEER_TOPK = 16
PEER_TOK_CHUNK = 128
EPS = 1e-6
IN_SIZES = (A_WIDTH, A_WIDTH, A_WIDTH, 2 * B_KW + B_VW, B_VW, B_HEADS, B_HEADS, D_MODEL, D_MODEL)
IN_TOTAL = 3 * A_WIDTH + 2 * B_KW + 2 * B_VW + 2 * B_HEADS + 2 * D_MODEL

kernel_name = 'hybrid_moba_gdn_peer_block'


def rms_norm(x, gain):
    xf = x.astype(jnp.float32)
    y = xf * lax.rsqrt(jnp.mean(xf * xf, axis=-1, keepdims=True) + EPS)
    return (y * gain.astype(jnp.float32)).astype(x.dtype)


def l2_norm(x):
    xf = x.astype(jnp.float32)
    return (xf * lax.rsqrt(jnp.sum(xf * xf, axis=-1, keepdims=True) + EPS)).astype(x.dtype)


def t5_bucket(rel):
    n = jnp.maximum(-rel, 0)
    max_exact = REL_BUCKETS // 2
    nf = jnp.maximum(n, 1).astype(jnp.float32)
    large = max_exact + (jnp.log(nf / max_exact) / math.log(REL_MAX_DIST / max_exact)
                         * (REL_BUCKETS - max_exact)).astype(jnp.int32)
    large = jnp.minimum(large, REL_BUCKETS - 1)
    return jnp.where(n < max_exact, n, large)


def moba_attention(q, k, v, rel_bias):
    B, H, S, dh = q.shape
    nb = -(-S // MOBA_BLOCK)
    pad = nb * MOBA_BLOCK - S
    kb = jnp.pad(k, ((0, 0), (0, 0), (0, pad), (0, 0))).reshape(B, H, nb, MOBA_BLOCK, dh)
    vb = jnp.pad(v, ((0, 0), (0, 0), (0, pad), (0, 0))).reshape(B, H, nb, MOBA_BLOCK, dh)
    kmean = jnp.mean(kb.astype(jnp.float32), axis=3)
    qblk = jnp.arange(S) // MOBA_BLOCK
    gate = jnp.einsum('bhsd,bhnd->bhsn', q.astype(jnp.float32), kmean)
    past = jnp.arange(nb)[None, :] < qblk[:, None]
    gate = jnp.where(past, gate, -jnp.inf)
    topk = min(MOBA_TOPK, nb)
    gsel, sel = lax.top_k(gate, topk)
    valid = jnp.isfinite(gsel)
    bi = jnp.arange(B)[:, None, None, None]
    hi = jnp.arange(H)[None, :, None, None]
    offs = jnp.arange(MOBA_BLOCK)
    scale = dh ** -0.5
    bias_hb = rel_bias.T.astype(jnp.float32)
    QC = MOBA_Q_CHUNK

    def one_chunk(c):
        start = c * QC
        qc = lax.dynamic_slice_in_dim(q, start, QC, axis=2)
        selc = lax.dynamic_slice_in_dim(sel, start, QC, axis=2)
        validc = lax.dynamic_slice_in_dim(valid, start, QC, axis=2)
        qpos = start + jnp.arange(QC)
        own = start // MOBA_BLOCK
        k_sel = kb[bi, hi, selc]
        v_sel = vb[bi, hi, selc]
        s_sel = jnp.einsum('bhqd,bhqkjd->bhqkj', qc, k_sel).astype(jnp.float32) * scale
        kpos_sel = selc[..., None] * MOBA_BLOCK + offs
        bias_sel = bias_hb[hi[..., None], t5_bucket(kpos_sel - qpos[None, None, :, None, None])]
        s_sel = jnp.where(validc[..., None], s_sel + bias_sel, -jnp.inf).reshape(B, H, QC, topk * MOBA_BLOCK)
        k_own = lax.dynamic_index_in_dim(kb, own, axis=2, keepdims=False)
        v_own = lax.dynamic_index_in_dim(vb, own, axis=2, keepdims=False)
        rel_own = (own * MOBA_BLOCK + offs)[None, :] - qpos[:, None]
        s_own = (jnp.einsum('bhqd,bhjd->bhqj', qc, k_own).astype(jnp.float32) * scale
                 + bias_hb[:, t5_bucket(rel_own)][None])
        s_own = jnp.where(rel_own <= 0, s_own, -jnp.inf)
        p = jax.nn.softmax(jnp.concatenate([s_sel, s_own], axis=-1), axis=-1)
        p_sel = p[..., :topk * MOBA_BLOCK].reshape(B, H, QC, topk, MOBA_BLOCK).astype(v.dtype)
        p_own = p[..., topk * MOBA_BLOCK:].astype(v.dtype)
        return (jnp.einsum('bhqkj,bhqkjd->bhqd', p_sel, v_sel)
                + jnp.einsum('bhqj,bhjd->bhqd', p_own, v_own))

    out = lax.map(one_chunk, jnp.arange(S // QC))
    return out.transpose(1, 2, 0, 3, 4).reshape(B, H, S, dh)


def causal_depthwise_conv(x, w):
    C = x.shape[-1]
    return lax.conv_general_dilated(x, w[:, None, :].astype(x.dtype), window_strides=(1,),
                                    padding=[(CONV_WIDTH - 1, 0)],
                                    dimension_numbers=('NWC', 'WIO', 'NWC'),
                                    feature_group_count=C)


def gated_delta_rule(q, k, v, g, beta):
    B, H, S, dk = q.shape
    dv = v.shape[-1]
    C = DELTA_CHUNK
    N = S // C
    f32 = jnp.float32
    qc = (q.astype(f32) * dk ** -0.5).reshape(B, H, N, C, dk)
    kc = k.astype(f32).reshape(B, H, N, C, dk)
    vc = v.astype(f32).reshape(B, H, N, C, dv)
    bc = beta.astype(f32).reshape(B, H, N, C)
    gcum = jnp.cumsum(g.astype(f32).reshape(B, H, N, C), axis=-1)
    idx = jnp.arange(C)
    causal = idx[:, None] >= idx[None, :]
    strict = idx[:, None] > idx[None, :]
    decay = jnp.exp(jnp.where(causal, gcum[..., :, None] - gcum[..., None, :], -jnp.inf))
    kbeta = kc * bc[..., None]
    vbeta = vc * bc[..., None]
    L = jnp.where(strict, jnp.einsum('bhnid,bhnjd->bhnij', kbeta, kc) * decay, 0.0)
    eye = jnp.eye(C, dtype=f32)
    T = lax.linalg.triangular_solve(eye + L, jnp.broadcast_to(eye, L.shape), left_side=True,
                                    lower=True, unit_diagonal=True)
    u = T @ vbeta
    w = T @ (kbeta * jnp.exp(gcum)[..., None])
    attn = jnp.einsum('bhnid,bhnjd->bhnij', qc, kc) * decay
    q_dec = qc * jnp.exp(gcum)[..., None]
    g_last = gcum[..., -1]
    k_dec = kc * jnp.exp(g_last[..., None] - gcum)[..., None]

    def step(state, inp):
        u_n, w_n, attn_n, qd_n, kd_n, gl_n = inp
        v_new = u_n - w_n @ state
        o = qd_n @ state + attn_n @ v_new
        state = state * jnp.exp(gl_n)[..., None, None] + jnp.swapaxes(kd_n, -1, -2) @ v_new
        return state, o

    xs = tuple(jnp.moveaxis(a, 2, 0) for a in (u, w, attn, q_dec, k_dec, g_last))
    s0 = jnp.zeros((B, H, dk, dv), f32)
    _, o = lax.scan(step, s0, xs)
    return o.transpose(1, 2, 0, 3, 4).reshape(B, H, S, dv)


def token_mixers(h, rel_bias, w_in, conv_w, a_log, dt_bias, q_norm_gain, k_norm_gain,
                 gdn_norm_gain, w_up_a, w_up_b, w_out):
    B, S, _ = h.shape
    proj = h @ w_in
    offsets = np.cumsum(IN_SIZES)[:-1].tolist()
    qa, ka, va, qkv_b, z_b, beta_raw, alpha_raw, gate_a, gate_b = jnp.split(proj, offsets, axis=-1)
    qa = rms_norm(qa.reshape(B, S, A_HEADS, A_HEAD_DIM), q_norm_gain).transpose(0, 2, 1, 3)
    ka = rms_norm(ka.reshape(B, S, A_HEADS, A_HEAD_DIM), k_norm_gain).transpose(0, 2, 1, 3)
    va = va.reshape(B, S, A_HEADS, A_HEAD_DIM).transpose(0, 2, 1, 3)
    o_a = moba_attention(qa, ka, va, rel_bias).transpose(0, 2, 1, 3).reshape(B, S, A_WIDTH)
    qkv_b = jax.nn.silu(causal_depthwise_conv(qkv_b, conv_w))
    qb, kb, vb = jnp.split(qkv_b, [B_KW, 2 * B_KW], axis=-1)
    qb = l2_norm(qb.reshape(B, S, B_HEADS, B_KEY_DIM)).transpose(0, 2, 1, 3)
    kb = l2_norm(kb.reshape(B, S, B_HEADS, B_KEY_DIM)).transpose(0, 2, 1, 3)
    vb = vb.reshape(B, S, B_HEADS, B_VAL_DIM).transpose(0, 2, 1, 3)
    beta = jax.nn.sigmoid(beta_raw.astype(jnp.float32)).transpose(0, 2, 1)
    g = (-jnp.exp(a_log.astype(jnp.float32))
         * jax.nn.softplus(alpha_raw.astype(jnp.float32) + dt_bias.astype(jnp.float32))).transpose(0, 2, 1)
    o_b = gated_delta_rule(qb, kb, vb, g, beta).astype(h.dtype).transpose(0, 2, 1, 3)
    o_b = rms_norm(o_b, gdn_norm_gain) * jax.nn.silu(z_b.reshape(B, S, B_HEADS, B_VAL_DIM))
    o_b = o_b.reshape(B, S, B_VW)
    merged = jax.nn.sigmoid(gate_a) * (o_a @ w_up_a) + jax.nn.sigmoid(gate_b) * (o_b @ w_up_b)
    return merged @ w_out


def peer_ffn(h, w_query, sub_keys, expert_down, expert_up):
    B, S, D = h.shape
    T = B * S
    K = PEER_TOPK
    ht = h.reshape(T, D)
    qry = (ht @ w_query).reshape(T, PEER_HEADS, 2, PEER_QDIM // 2)
    scores = jnp.einsum('thpd,hpkd->thpk', qry, sub_keys).astype(jnp.float32)
    s_top, i_top = lax.top_k(scores, K)
    cand = (s_top[:, :, 0, :, None] + s_top[:, :, 1, None, :]).reshape(T, PEER_HEADS, K * K)
    cand_idx = (i_top[:, :, 0, :, None] * PEER_NKEYS + i_top[:, :, 1, None, :]).reshape(T, PEER_HEADS, K * K)
    best, pos = lax.top_k(cand, K)
    expert_idx = jnp.take_along_axis(cand_idx, pos, axis=-1)
    gate = jax.nn.softmax(best, axis=-1)
    TC = PEER_TOK_CHUNK

    def one_chunk(c):
        start = c * TC
        hc = lax.dynamic_slice_in_dim(ht, start, TC, axis=0)
        ec = lax.dynamic_slice_in_dim(expert_idx, start, TC, axis=0)
        gc = lax.dynamic_slice_in_dim(gate, start, TC, axis=0)
        u = expert_down[ec]
        vv = expert_up[ec]
        act = jax.nn.gelu(jnp.einsum('td,thkd->thk', hc, u).astype(jnp.float32), approximate=False)
        return jnp.einsum('thk,thkd->td', (gc * act).astype(vv.dtype), vv)

    out = lax.map(one_chunk, jnp.arange(T // TC))
    return out.reshape(B, S, D)


def setup_inputs(seed: int = 0) -> dict:
    key = jax.random.key(seed)
    ks = jax.random.split(key, 20)
    f32 = jnp.float32

    def nrm(k, shape, scale):
        return jax.random.normal(k, shape, f32) * scale

    def gain(k, n):
        return 1.0 + 0.05 * jax.random.normal(k, (DEPTH, n), f32)

    dt = jnp.exp(jax.random.uniform(ks[6], (DEPTH, B_HEADS), f32, math.log(1e-3), math.log(1e-1)))
    return {
        'x': nrm(ks[0], (BATCH, SEQ, D_MODEL), 1.0),
        'rel_bias': nrm(ks[1], (REL_BUCKETS, A_HEADS), 0.5),
        'norm1_gain': gain(ks[2], D_MODEL),
        'w_in': nrm(ks[3], (DEPTH, D_MODEL, IN_TOTAL), D_MODEL ** -0.5),
        'conv_w': nrm(ks[4], (DEPTH, CONV_WIDTH, 2 * B_KW + B_VW), CONV_WIDTH ** -0.5),
        'a_log': jnp.log(jax.random.uniform(ks[5], (DEPTH, B_HEADS), f32, 1.0, 16.0)),
        'dt_bias': jnp.log(jnp.expm1(dt)),
        'q_norm_gain': gain(ks[7], A_HEAD_DIM),
        'k_norm_gain': gain(ks[8], A_HEAD_DIM),
        'gdn_norm_gain': gain(ks[9], B_VAL_DIM),
        'w_up_a': nrm(ks[10], (DEPTH, A_WIDTH, D_MODEL), A_WIDTH ** -0.5),
        'w_up_b': nrm(ks[11], (DEPTH, B_VW, D_MODEL), B_VW ** -0.5),
        'w_out': nrm(ks[12], (DEPTH, D_MODEL, D_MODEL), D_MODEL ** -0.5),
        'norm2_gain': gain(ks[13], D_MODEL),
        'peer_w_query': nrm(ks[14], (DEPTH, D_MODEL, PEER_HEADS * PEER_QDIM), D_MODEL ** -0.5),
        'peer_sub_keys': nrm(ks[15], (DEPTH, PEER_HEADS, 2, PEER_NKEYS, PEER_QDIM // 2), (PEER_QDIM // 2) ** -0.5),
        'peer_u': nrm(ks[16], (DEPTH, PEER_EXPERTS, D_MODEL), D_MODEL ** -0.5),
        'peer_v': nrm(ks[17], (DEPTH, PEER_EXPERTS, D_MODEL), PEER_HEADS ** -0.5),
    }


def reference(x, rel_bias, norm1_gain, w_in, conv_w, a_log, dt_bias, q_norm_gain, k_norm_gain,
              gdn_norm_gain, w_up_a, w_up_b, w_out, norm2_gain, peer_w_query, peer_sub_keys,
              peer_u, peer_v):
    for l in range(DEPTH):
        h = rms_norm(x, norm1_gain[l])
        x = x + token_mixers(h, rel_bias, w_in[l], conv_w[l], a_log[l], dt_bias[l], q_norm_gain[l],
                             k_norm_gain[l], gdn_norm_gain[l], w_up_a[l], w_up_b[l], w_out[l])
        h2 = rms_norm(x, norm2_gain[l])
        x = x + peer_ffn(h2, peer_w_query[l], peer_sub_keys[l], peer_u[l], peer_v[l])
    return x
```

```python
import functools
import math

import jax
import jax.numpy as jnp
import numpy as np
from jax import lax
from jax.experimental import pallas as pl
from jax.experimental.pallas import tpu as pltpu

F32 = jnp.float32
BF16 = jnp.bfloat16
HIGHEST = lax.Precision.HIGHEST

LANES = 128
EPS = 1e-6

A_HEADS = 8
A_HEAD_DIM = 128
MOBA_BLOCK = 256
MOBA_TOPK = 3
REL_BUCKETS = 32
REL_MAX_DIST = 128
B_HEADS = 8
B_KEY_DIM = 128
B_VAL_DIM = 128
CONV_WIDTH = 4
DELTA_CHUNK = 64
GDN_GROUP = 256
PEER_HEADS = 8
PEER_NKEYS = 128
PEER_QDIM = 256
PEER_TOPK = 16

VMEM_LIMIT = 56 * 1024 * 1024


def _cparams(sem):
    return pltpu.CompilerParams(dimension_semantics=sem, vmem_limit_bytes=VMEM_LIMIT)


def _dot(a, b):
    return jnp.dot(a.astype(BF16), b.astype(BF16), preferred_element_type=F32)


def _dot_nt(a, b):
    return lax.dot_general(a.astype(BF16), b.astype(BF16), (((1,), (1,)), ((), ())),
                           preferred_element_type=F32)


def _dot_tn(a, b):
    return lax.dot_general(a.astype(BF16), b.astype(BF16), (((0,), (0,)), ((), ())),
                           preferred_element_type=F32)


def _dot_hi(a, b):
    return jnp.dot(a, b, preferred_element_type=F32, precision=HIGHEST)


def _dot_nt_hi(a, b):
    return lax.dot_general(a, b, (((1,), (1,)), ((), ())), preferred_element_type=F32,
                           precision=HIGHEST)


def _sigmoid(x):
    return 1.0 / (1.0 + jnp.exp(-x))


def _silu(x):
    return x * _sigmoid(x)


def _in_proj_kernel(x_ref, g_ref, w_ref, ws_ref, o_ref, os_ref, h_scr):
    @pl.when(pl.program_id(1) == 0)
    def _():
        x = x_ref[...]
        ms = jnp.mean(x * x, axis=-1, keepdims=True)
        h = (x * lax.rsqrt(ms + EPS) * g_ref[...]).astype(BF16)
        h_scr[...] = h
        os_ref[...] = jnp.dot(h, ws_ref[...], preferred_element_type=F32)

    o_ref[...] = jnp.dot(h_scr[...], w_ref[...], preferred_element_type=F32)


def _in_proj(x2, gain, w_main, w_small, tm=512, tn=1024):
    T, D = x2.shape
    N = w_main.shape[1]
    return pl.pallas_call(
        _in_proj_kernel,
        grid=(T // tm, N // tn),
        in_specs=[
            pl.BlockSpec((tm, D), lambda i, j: (i, 0)),
            pl.BlockSpec((1, D), lambda i, j: (0, 0)),
            pl.BlockSpec((D, tn), lambda i, j: (0, j)),
            pl.BlockSpec((D, LANES), lambda i, j: (0, 0)),
        ],
        out_specs=[
            pl.BlockSpec((tm, tn), lambda i, j: (i, j)),
            pl.BlockSpec((tm, LANES), lambda i, j: (i, 0)),
        ],
        out_shape=[jax.ShapeDtypeStruct((T, N), F32), jax.ShapeDtypeStruct((T, LANES), F32)],
        scratch_shapes=[pltpu.VMEM((tm, D), BF16)],
        compiler_params=_cparams(("parallel", "arbitrary")),
        name="in_proj",
    )(x2, gain, w_main, w_small)


def _moba_kernel(q_ref, k_ref, v_ref, qg_ref, kg_ref, t0_ref, t1_ref, far_ref, o_ref, *, nb):
    BS = MOBA_BLOCK
    scale = A_HEAD_DIM ** -0.5
    q = q_ref[...]
    k = k_ref[...]
    qn = q * lax.rsqrt(jnp.mean(q * q, axis=-1, keepdims=True) + EPS) * qg_ref[...]
    kn = k * lax.rsqrt(jnp.mean(k * k, axis=-1, keepdims=True) + EPS) * kg_ref[...]
    kmean = jnp.concatenate(
        [jnp.mean(kn[j * BS:(j + 1) * BS], axis=0, keepdims=True) for j in range(nb)], axis=0)
    knb = kn.astype(BF16)
    vb = v_ref[...].astype(BF16)
    t0 = t0_ref[...]
    t1 = t1_ref[...]
    far = far_ref[0:1, :]
    rows = lax.broadcasted_iota(jnp.int32, (BS, BS), 0)
    cols = lax.broadcasted_iota(jnp.int32, (BS, BS), 1)
    causal = cols <= rows
    blk_iota = lax.broadcasted_iota(jnp.int32, (BS, nb), 1)
    neg = -jnp.inf
    for qi in range(nb):
        qblk = qn[qi * BS:(qi + 1) * BS]
        sel = jnp.zeros((BS, nb), dtype=jnp.bool_)
        if qi > 0:
            gate = _dot_nt_hi(qblk, kmean)
            gate = jnp.where(blk_iota < qi, gate, neg)
            for _ in range(min(MOBA_TOPK, nb)):
                m = jnp.max(gate, axis=1, keepdims=True)
                idx = jnp.min(jnp.where(gate == m, blk_iota, nb), axis=1, keepdims=True)
                hit = (blk_iota == idx) & (m > neg)
                sel = sel | hit
                gate = jnp.where(blk_iota == idx, neg, gate)
        self32 = sel.astype(F32)
        s_all = _dot_nt(qblk, knb[:(qi + 1) * BS]) * scale
        blocks = []
        for j in range(qi + 1):
            s = s_all[:, j * BS:(j + 1) * BS]
            if j == qi:
                s = jnp.where(causal, s + t0, neg)
            else:
                bias = t1 if j == qi - 1 else far
                picked = self32[:, j:j + 1] > 0.5
                s = jnp.where(picked, s + bias, neg)
            blocks.append(s)
        m = jnp.max(blocks[-1], axis=1, keepdims=True)
        for s in blocks[:-1]:
            m = jnp.maximum(m, jnp.max(s, axis=1, keepdims=True))
        l = jnp.zeros((BS, 1), F32)
        acc = jnp.zeros((BS, A_HEAD_DIM), F32)
        for j, s in enumerate(blocks):
            p = jnp.exp(s - m)
            l = l + jnp.sum(p, axis=1, keepdims=True)
            acc = acc + _dot(p, vb[j * BS:(j + 1) * BS])
        o_ref[qi * BS:(qi + 1) * BS, :] = (acc / l).astype(o_ref.dtype)


def _moba(proj, q_gain, k_gain, t0, t1, far, B, S):
    H = A_HEADS
    nb = S // MOBA_BLOCK
    dh = A_HEAD_DIM
    return pl.pallas_call(
        functools.partial(_moba_kernel, nb=nb),
        grid=(B, H),
        in_specs=[
            pl.BlockSpec((S, dh), lambda b, h: (b, h)),
            pl.BlockSpec((S, dh), lambda b, h: (b, H + h)),
            pl.BlockSpec((S, dh), lambda b, h: (b, 2 * H + h)),
            pl.BlockSpec((1, dh), lambda b, h: (0, 0)),
            pl.BlockSpec((1, dh), lambda b, h: (0, 0)),
            pl.BlockSpec((None, MOBA_BLOCK, MOBA_BLOCK), lambda b, h: (h, 0, 0)),
            pl.BlockSpec((None, MOBA_BLOCK, MOBA_BLOCK), lambda b, h: (h, 0, 0)),
            pl.BlockSpec((None, 8, MOBA_BLOCK), lambda b, h: (h, 0, 0)),
        ],
        out_specs=pl.BlockSpec((S, dh), lambda b, h: (b, h)),
        out_shape=jax.ShapeDtypeStruct((B * S, H * dh), BF16),
        compiler_params=_cparams(("parallel", "parallel")),
        name="moba",
    )(proj, proj, proj, q_gain, k_gain, t0, t1, far)


def _t5_bucket(rel):
    n = jnp.maximum(-rel, 0)
    max_exact = REL_BUCKETS // 2
    nf = jnp.maximum(n, 1).astype(F32)
    large = max_exact + (jnp.log(nf / max_exact) / math.log(REL_MAX_DIST / max_exact)
                         * (REL_BUCKETS - max_exact)).astype(jnp.int32)
    large = jnp.minimum(large, REL_BUCKETS - 1)
    return jnp.where(n < max_exact, n, large)


def _bias_tiles(rel_bias):
    assert MOBA_BLOCK + 1 >= REL_MAX_DIST
    offs = jnp.arange(MOBA_BLOCK)
    rel0 = offs[None, :] - offs[:, None]
    bias_hb = rel_bias.T.astype(F32)
    t0 = bias_hb[:, _t5_bucket(rel0)]
    t1 = bias_hb[:, _t5_bucket(rel0 - MOBA_BLOCK)]
    far = jnp.broadcast_to(bias_hb[:, REL_BUCKETS - 1][:, None, None], (A_HEADS, 8, MOBA_BLOCK))
    return t0, t1, far


def _gdn_kernel(q_ref, k_ref, v_ref, z_ref, sm_ref, cwq_ref, cwk_ref, cwv_ref, alog_ref, dtb_ref,
                ng_ref, o_ref, qs, ks, vs, bs, gs, *, S):
    C = DELTA_CHUNK
    G = GDN_GROUP
    dk = B_KEY_DIM
    hh = pl.program_id(1)
    row = lax.broadcasted_iota(jnp.int32, (S, dk), 0)

    def conv_silu(x, w):
        y = x * w[CONV_WIDTH - 1:CONV_WIDTH, :]
        for d in range(1, CONV_WIDTH):
            xs = jnp.where(row >= d, pltpu.roll(x, d, 0), 0.0)
            y = y + xs * w[CONV_WIDTH - 1 - d:CONV_WIDTH - d, :]
        return _silu(y)

    def l2n(x):
        return x * lax.rsqrt(jnp.sum(x * x, axis=-1, keepdims=True) + EPS)

    qs[...] = l2n(conv_silu(q_ref[...], cwq_ref[...])) * (dk ** -0.5)
    ks[...] = l2n(conv_silu(k_ref[...], cwk_ref[...]))
    vs[...] = conv_silu(v_ref[...], cwv_ref[...])

    sm = sm_ref[...]
    beta_all = _sigmoid(sm)
    z = sm + dtb_ref[...]
    softplus = jnp.maximum(z, 0.0) + jnp.log(1.0 + jnp.exp(-jnp.abs(z)))
    g_all = -jnp.exp(alog_ref[...]) * softplus
    sel_r = lax.broadcasted_iota(jnp.int32, (LANES, LANES), 0)
    bs[...] = _dot_hi(beta_all, (sel_r == hh).astype(F32))
    gs[...] = _dot_hi(g_all, (sel_r == hh + B_HEADS).astype(F32))

    ri = lax.broadcasted_iota(jnp.int32, (G, G), 0)
    ci = lax.broadcasted_iota(jnp.int32, (G, G), 1)
    same = (ri // C) == (ci // C)
    strict = same & (ri > ci)
    causal = same & (ri >= ci)
    tri = causal.astype(F32)
    strict_f = strict.astype(F32)
    eye = (ri == ci).astype(F32)
    ng = ng_ref[...]

    def group(gi, state):
        r0 = pl.multiple_of(gi * G, G)
        qg = qs[pl.ds(r0, G), :]
        kg = ks[pl.ds(r0, G), :]
        vg = vs[pl.ds(r0, G), :]
        bg = bs[pl.ds(r0, G), :]
        gg = gs[pl.ds(r0, G), :]
        gmat = jnp.concatenate([gg, gg], axis=1)
        dmat = _dot_hi(tri, gmat * strict_f)
        gcum = _dot_hi(tri, gg)
        decay = jnp.where(causal, jnp.exp(jnp.where(causal, dmat, 0.0)), 0.0)
        kb = kg * bg
        vb = vg * bg
        kk = _dot_nt_hi(kb, kg)
        a = -(kk * decay) * strict_f
        t = eye + a
        p = a
        for _ in range(int(math.log2(C)) - 1):
            p = _dot_hi(p, p)
            t = t + _dot_hi(t, p)
        eg = jnp.exp(gcum)
        u = _dot_hi(t, vb)
        w = _dot_hi(t, kb * eg)
        attn = _dot_nt_hi(qg, kg) * decay
        qd = qg * eg
        outs = []
        vnews = []
        for c in range(G // C):
            sl = slice(c * C, (c + 1) * C)
            glast = gcum[(c + 1) * C - 1:(c + 1) * C, :]
            kd = kg[sl] * jnp.exp(glast - gcum[sl])
            v_new = u[sl] - _dot_hi(w[sl], state)
            outs.append(_dot_hi(qd[sl], state))
            vnews.append(v_new)
            state = state * jnp.exp(glast)[:, 0:1] + lax.dot_general(
                kd, v_new, (((0,), (0,)), ((), ())), preferred_element_type=F32, precision=HIGHEST)
        o = jnp.concatenate(outs, axis=0) + _dot_hi(attn, jnp.concatenate(vnews, axis=0))
        on = o * lax.rsqrt(jnp.mean(o * o, axis=-1, keepdims=True) + EPS) * ng
        o_ref[pl.ds(r0, G), :] = (on * _silu(z_ref[pl.ds(r0, G), :])).astype(o_ref.dtype)
        return state

    lax.fori_loop(0, S // G, group, jnp.zeros((dk, B_VAL_DIM), F32))


def _gdn(proj, small, conv_w, alog_pad, dtb_pad, norm_gain, B, S):
    H = B_HEADS
    dk = B_KEY_DIM
    qoff = 3 * A_HEADS
    blk = lambda off: pl.BlockSpec((S, dk), lambda b, h: (b, off + h))
    cw = lambda off: pl.BlockSpec((CONV_WIDTH, dk), lambda b, h: (0, off + h))
    one = pl.BlockSpec((1, LANES), lambda b, h: (0, 0))
    return pl.pallas_call(
        functools.partial(_gdn_kernel, S=S),
        grid=(B, H),
        in_specs=[blk(qoff), blk(qoff + H), blk(qoff + 2 * H), blk(qoff + 3 * H),
                  pl.BlockSpec((S, LANES), lambda b, h: (b, 0)),
                  cw(0), cw(H), cw(2 * H), one, one, one],
        out_specs=pl.BlockSpec((S, dk), lambda b, h: (b, h)),
        out_shape=jax.ShapeDtypeStruct((B * S, H * B_VAL_DIM), BF16),
        scratch_shapes=[pltpu.VMEM((S, dk), F32)] * 5,
        compiler_params=_cparams(("parallel", "parallel")),
        name="gdn",
    )(proj, proj, proj, proj, small, conv_w, conv_w, conv_w, alog_pad, dtb_pad, norm_gain)


def _merge_kernel(oa_ref, ob_ref, wa_ref, wb_ref, ga_ref, gb_ref, o_ref):
    a = jnp.dot(oa_ref[...], wa_ref[...], preferred_element_type=F32)
    b = jnp.dot(ob_ref[...], wb_ref[...], preferred_element_type=F32)
    o_ref[...] = (_sigmoid(ga_ref[...]) * a + _sigmoid(gb_ref[...]) * b).astype(o_ref.dtype)


def _merge(o_a, o_b, w_up_a, w_up_b, proj, tm=512, tn=1024):
    T, K = o_a.shape
    D = w_up_a.shape[1]
    ga_off = (3 * A_HEADS + 4 * B_HEADS) * LANES // tn
    gb_off = ga_off + D // tn
    return pl.pallas_call(
        _merge_kernel,
        grid=(T // tm, D // tn),
        in_specs=[
            pl.BlockSpec((tm, K), lambda i, j: (i, 0)),
            pl.BlockSpec((tm, K), lambda i, j: (i, 0)),
            pl.BlockSpec((K, tn), lambda i, j: (0, j)),
            pl.BlockSpec((K, tn), lambda i, j: (0, j)),
            pl.BlockSpec((tm, tn), lambda i, j: (i, ga_off + j)),
            pl.BlockSpec((tm, tn), lambda i, j: (i, gb_off + j)),
        ],
        out_specs=pl.BlockSpec((tm, tn), lambda i, j: (i, j)),
        out_shape=jax.ShapeDtypeStruct((T, D), BF16),
        compiler_params=_cparams(("parallel", "parallel")),
        name="merge",
    )(o_a, o_b, w_up_a, w_up_b, proj, proj)


def _out_proj_kernel(m_ref, w_ref, x_ref, g_ref, y_ref, h_ref):
    y = x_ref[...] + jnp.dot(m_ref[...], w_ref[...], preferred_element_type=F32)
    y_ref[...] = y
    ms = jnp.mean(y * y, axis=-1, keepdims=True)
    h_ref[...] = (y * lax.rsqrt(ms + EPS) * g_ref[...]).astype(h_ref.dtype)


def _out_proj(merged, w_out, x2, gain2, tm=512):
    T, D = x2.shape
    return pl.pallas_call(
        _out_proj_kernel,
        grid=(T // tm,),
        in_specs=[
            pl.BlockSpec((tm, D), lambda i: (i, 0)),
            pl.BlockSpec((D, D), lambda i: (0, 0)),
            pl.BlockSpec((tm, D), lambda i: (i, 0)),
            pl.BlockSpec((1, D), lambda i: (0, 0)),
        ],
        out_specs=[pl.BlockSpec((tm, D), lambda i: (i, 0)), pl.BlockSpec((tm, D), lambda i: (i, 0))],
        out_shape=[jax.ShapeDtypeStruct((T, D), F32), jax.ShapeDtypeStruct((T, D), BF16)],
        compiler_params=_cparams(("parallel",)),
        name="out_proj",
    )(merged, w_out, x2, gain2)


def _matmul_kernel(a_ref, b_ref, o_ref):
    o_ref[...] = jnp.dot(a_ref[...], b_ref[...], preferred_element_type=F32)


def _peer_query(h2, w_query, tm=512):
    T, D = h2.shape
    N = w_query.shape[1]
    return pl.pallas_call(
        _matmul_kernel,
        grid=(T // tm,),
        in_specs=[pl.BlockSpec((tm, D), lambda i: (i, 0)), pl.BlockSpec((D, N), lambda i: (0, 0))],
        out_specs=pl.BlockSpec((tm, N), lambda i: (i, 0)),
        out_shape=jax.ShapeDtypeStruct((T, N), F32),
        compiler_params=_cparams(("parallel",)),
        name="peer_query",
    )(h2, w_query)


def _topk_rows(s, k, pos):
    n = s.shape[0]
    vals, idxs = [], []
    for _ in range(k):
        m = jnp.max(s, axis=0, keepdims=True)
        i = jnp.min(jnp.where(s == m, pos, n), axis=0, keepdims=True)
        vals.append(m)
        idxs.append(i)
        s = jnp.where(pos == i, -jnp.inf, s)
    return jnp.concatenate(vals, axis=0), jnp.concatenate(idxs, axis=0)


def _peer_route_kernel(q_ref, keys_ref, idx_ref, gate_ref, *, tt):
    K = PEER_TOPK
    half = PEER_QDIM // 2
    pos_k = lax.broadcasted_iota(jnp.int32, (PEER_NKEYS, tt), 0)
    pos_c = lax.broadcasted_iota(jnp.int32, (K * K, tt), 0)
    idx_rows, gate_rows = [], []
    for h in range(PEER_HEADS):
        tops = []
        for p in range(2):
            qh = q_ref[:, (2 * h + p) * half:(2 * h + p + 1) * half]
            sc = _dot_nt_hi(keys_ref[h, p], qh)
            tops.append(_topk_rows(sc, K, pos_k))
        (s0, i0), (s1, i1) = tops
        cand = jnp.concatenate([s0[i:i + 1] + s1 for i in range(K)], axis=0)
        cidx = jnp.concatenate([i0[i:i + 1] * PEER_NKEYS + i1 for i in range(K)], axis=0)
        best, bpos = _topk_rows(cand, K, pos_c)
        eidx = jnp.concatenate(
            [jnp.min(jnp.where(pos_c == bpos[r:r + 1], cidx, PEER_NKEYS * PEER_NKEYS),
                     axis=0, keepdims=True) for r in range(K)], axis=0)
        e = jnp.exp(best - best[0:1])
        gate_rows.append(e / jnp.sum(e, axis=0, keepdims=True))
        idx_rows.append(eidx)
    idx_ref[...] = jnp.concatenate(idx_rows, axis=0).T
    gate_ref[...] = jnp.concatenate(gate_rows, axis=0).T


def _peer_route(qry, sub_keys, tt=256):
    T, N = qry.shape
    PK = PEER_HEADS * PEER_TOPK
    return pl.pallas_call(
        functools.partial(_peer_route_kernel, tt=tt),
        grid=(T // tt,),
        in_specs=[pl.BlockSpec((tt, N), lambda i: (i, 0)),
                  pl.BlockSpec(sub_keys.shape, lambda i: (0, 0, 0, 0))],
        out_specs=[pl.BlockSpec((tt, PK), lambda i: (i, 0)), pl.BlockSpec((tt, PK), lambda i: (i, 0))],
        out_shape=[jax.ShapeDtypeStruct((T, PK), jnp.int32), jax.ShapeDtypeStruct((T, PK), F32)],
        compiler_params=_cparams(("parallel",)),
        name="peer_route",
    )(qry, sub_keys)


PEER_SLOTS = 4
PEER_TOKEN_GROUP = 16


def _gelu(x):
    return 0.5 * x * (1.0 + lax.erf(x * (2.0 ** -0.5)))


def _peer_expert_kernel(idx_ref, gate_ref, h_ref, y_ref, tab_ref, o_ref, buf, sem, *, tt, D):
    PK = PEER_HEADS * PEER_TOPK
    NS = PEER_SLOTS

    def issue(t, slot):
        def body(k, c):
            e = idx_ref[t, k]
            pltpu.make_async_copy(tab_ref.at[pl.ds(e, 1)], buf.at[slot, pl.ds(k, 1)],
                                  sem.at[slot]).start()
            return c
        lax.fori_loop(0, PK, body, 0, unroll=8)

    def wait(slot):
        pltpu.make_async_copy(tab_ref.at[pl.ds(0, PK)], buf.at[slot], sem.at[slot]).wait()

    for t in range(NS - 1):
        issue(t, t)

    GT = PEER_TOKEN_GROUP

    def group(gi, c):
        t0 = pl.multiple_of(gi * GT, GT)
        hg = h_ref[pl.ds(t0, GT), :]
        gg = gate_ref[pl.ds(t0, GT), :]
        outs = []
        for j in range(GT):
            slot = j % NS
            nxt = t0 + j + NS - 1

            @pl.when(nxt < tt)
            def _():
                issue(nxt, (j + NS - 1) % NS)

            wait(slot)
            rows = buf[slot]
            s = _dot_nt(hg, rows[:, :D])[j:j + 1]
            a = _gelu(s) * gg[j:j + 1]
            outs.append(_dot(a, rows[:, D:]))
        o_ref[pl.ds(t0, GT), :] = y_ref[pl.ds(t0, GT), :] + jnp.concatenate(outs, axis=0)
        return c

    lax.fori_loop(0, tt // GT, group, 0)


def _peer_experts(idx, gate, h2, y1, table, tt=64):
    T, D = y1.shape
    PK = PEER_HEADS * PEER_TOPK
    return pl.pallas_call(
        functools.partial(_peer_expert_kernel, tt=tt, D=D),
        grid=(T // tt,),
        in_specs=[
            pl.BlockSpec((tt, PK), lambda i: (i, 0), memory_space=pltpu.SMEM),
            pl.BlockSpec((tt, PK), lambda i: (i, 0)),
            pl.BlockSpec((tt, D), lambda i: (i, 0)),
            pl.BlockSpec((tt, D), lambda i: (i, 0)),
            pl.BlockSpec(memory_space=pl.ANY),
        ],
        out_specs=pl.BlockSpec((tt, D), lambda i: (i, 0)),
        out_shape=jax.ShapeDtypeStruct((T, D), F32),
        scratch_shapes=[pltpu.VMEM((PEER_SLOTS, PK, 2 * D), F32),
                        pltpu.SemaphoreType.DMA((PEER_SLOTS,))],
        compiler_params=_cparams(("arbitrary",)),
        name="peer_experts",
    )(idx, gate, h2, y1, table)


def _layer(x, rel_bias, norm1_gain, w_in, conv_w, a_log, dt_bias, q_norm_gain, k_norm_gain,
           gdn_norm_gain, w_up_a, w_up_b, w_out, norm2_gain, peer_w_query, peer_sub_keys,
           peer_u, peer_v):
    B, S, D = x.shape
    T = B * S
    x2 = x.reshape(T, D)
    n_main = 3 * A_HEADS * A_HEAD_DIM + 4 * B_HEADS * B_KEY_DIM
    n_small = 2 * B_HEADS
    w_main = jnp.concatenate([w_in[:, :n_main], w_in[:, n_main + n_small:]], axis=1).astype(BF16)
    w_small = jnp.pad(w_in[:, n_main:n_main + n_small], ((0, 0), (0, LANES - n_small))).astype(BF16)
    proj, small = _in_proj(x2, norm1_gain[None, :], w_main, w_small)

    t0, t1, far = _bias_tiles(rel_bias)
    o_a = _moba(proj, q_norm_gain[None, :], k_norm_gain[None, :], t0, t1, far, B, S)

    pad8 = lambda v: jnp.pad(v, (B_HEADS, LANES - 2 * B_HEADS))[None, :]
    o_b = _gdn(proj, small, conv_w, pad8(a_log), pad8(dt_bias), gdn_norm_gain[None, :], B, S)

    merged = _merge(o_a, o_b, w_up_a.astype(BF16), w_up_b.astype(BF16), proj)
    y1, h2 = _out_proj(merged, w_out.astype(BF16), x2, norm2_gain[None, :])

    qry = _peer_query(h2, peer_w_query.astype(BF16))
    idx, gate = _peer_route(qry, peer_sub_keys)
    table = jnp.concatenate([peer_u, peer_v], axis=1)
    y = _peer_experts(idx, gate, h2, y1, table)
    return y.reshape(B, S, D)


def kernel(x, rel_bias, norm1_gain, w_in, conv_w, a_log, dt_bias, q_norm_gain, k_norm_gain,
           gdn_norm_gain, w_up_a, w_up_b, w_out, norm2_gain, peer_w_query, peer_sub_keys,
           peer_u, peer_v):
    depth = w_in.shape[0]
    for l in range(depth):
        x = _layer(x, rel_bias, norm1_gain[l], w_in[l], conv_w[l], a_log[l], dt_bias[l],
                   q_norm_gain[l], k_norm_gain[l], gdn_norm_gain[l], w_up_a[l], w_up_b[l],
                   w_out[l], norm2_gain[l], peer_w_query[l], peer_sub_keys[l], peer_u[l], peer_v[l])
    return x
```

```python
import functools
import math

import jax
import jax.numpy as jnp
import numpy as np
from jax import lax
from jax.experimental import pallas as pl
from jax.experimental.pallas import tpu as pltpu

F32 = jnp.float32
BF16 = jnp.bfloat16
HIGHEST = lax.Precision.HIGHEST

LANES = 128
EPS = 1e-6

A_HEADS = 8
A_HEAD_DIM = 128
MOBA_BLOCK = 256
MOBA_TOPK = 3
REL_BUCKETS = 32
REL_MAX_DIST = 128
B_HEADS = 8
B_KEY_DIM = 128
B_VAL_DIM = 128
CONV_WIDTH = 4
DELTA_CHUNK = 64
GDN_GROUP = 256
PEER_HEADS = 8
PEER_NKEYS = 128
PEER_QDIM = 256
PEER_TOPK = 16

VMEM_LIMIT = 56 * 1024 * 1024


def _cparams(sem):
    return pltpu.CompilerParams(dimension_semantics=sem, vmem_limit_bytes=VMEM_LIMIT)


def _dot(a, b):
    return jnp.dot(a.astype(BF16), b.astype(BF16), preferred_element_type=F32)


def _dot_nt(a, b):
    return lax.dot_general(a.astype(BF16), b.astype(BF16), (((1,), (1,)), ((), ())),
                           preferred_element_type=F32)


def _dot_tn(a, b):
    return lax.dot_general(a.astype(BF16), b.astype(BF16), (((0,), (0,)), ((), ())),
                           preferred_element_type=F32)


def _dot_hi(a, b):
    return jnp.dot(a, b, preferred_element_type=F32, precision=HIGHEST)


def _dot_nt_hi(a, b):
    return lax.dot_general(a, b, (((1,), (1,)), ((), ())), preferred_element_type=F32,
                           precision=HIGHEST)


def _sigmoid(x):
    return 1.0 / (1.0 + jnp.exp(-x))


def _silu(x):
    return x * _sigmoid(x)


def _in_proj_kernel(x_ref, g_ref, w_ref, ws_ref, o_ref, os_ref, h_scr):
    @pl.when(pl.program_id(1) == 0)
    def _():
        x = x_ref[...]
        ms = jnp.mean(x * x, axis=-1, keepdims=True)
        h = (x * lax.rsqrt(ms + EPS) * g_ref[...]).astype(BF16)
        h_scr[...] = h
        os_ref[...] = jnp.dot(h, ws_ref[...], preferred_element_type=F32)

    o_ref[...] = jnp.dot(h_scr[...], w_ref[...], preferred_element_type=F32)


def _in_proj(x2, gain, w_main, w_small, tm=512, tn=1024):
    T, D = x2.shape
    N = w_main.shape[1]
    return pl.pallas_call(
        _in_proj_kernel,
        grid=(T // tm, N // tn),
        in_specs=[
            pl.BlockSpec((tm, D), lambda i, j: (i, 0)),
            pl.BlockSpec((1, D), lambda i, j: (0, 0)),
            pl.BlockSpec((D, tn), lambda i, j: (0, j)),
            pl.BlockSpec((D, LANES), lambda i, j: (0, 0)),
        ],
        out_specs=[
            pl.BlockSpec((tm, tn), lambda i, j: (i, j)),
            pl.BlockSpec((tm, LANES), lambda i, j: (i, 0)),
        ],
        out_shape=[jax.ShapeDtypeStruct((T, N), F32), jax.ShapeDtypeStruct((T, LANES), F32)],
        scratch_shapes=[pltpu.VMEM((tm, D), BF16)],
        compiler_params=_cparams(("parallel", "arbitrary")),
        name="in_proj",
    )(x2, gain, w_main, w_small)


def _moba_kernel(q_ref, k_ref, v_ref, qg_ref, kg_ref, t0_ref, t1_ref, far_ref, o_ref, *, nb):
    BS = MOBA_BLOCK
    scale = A_HEAD_DIM ** -0.5
    q = q_ref[...]
    k = k_ref[...]
    qn = q * lax.rsqrt(jnp.mean(q * q, axis=-1, keepdims=True) + EPS) * qg_ref[...]
    kn = k * lax.rsqrt(jnp.mean(k * k, axis=-1, keepdims=True) + EPS) * kg_ref[...]
    kmean = jnp.concatenate(
        [jnp.mean(kn[j * BS:(j + 1) * BS], axis=0, keepdims=True) for j in range(nb)], axis=0)
    knb = kn.astype(BF16)
    vb = v_ref[...].astype(BF16)
    t0 = t0_ref[...]
    t1 = t1_ref[...]
    far = far_ref[0:1, :]
    rows = lax.broadcasted_iota(jnp.int32, (BS, BS), 0)
    cols = lax.broadcasted_iota(jnp.int32, (BS, BS), 1)
    causal = cols <= rows
    blk_iota = lax.broadcasted_iota(jnp.int32, (BS, nb), 1)
    neg = -jnp.inf
    for qi in range(nb):
        qblk = qn[qi * BS:(qi + 1) * BS]
        sel = jnp.zeros((BS, nb), dtype=jnp.bool_)
        if qi > 0:
            gate = _dot_nt_hi(qblk, kmean)
            gate = jnp.where(blk_iota < qi, gate, neg)
            for _ in range(min(MOBA_TOPK, nb)):
                m = jnp.max(gate, axis=1, keepdims=True)
                idx = jnp.min(jnp.where(gate == m, blk_iota, nb), axis=1, keepdims=True)
                hit = (blk_iota == idx) & (m > neg)
                sel = sel | hit
                gate = jnp.where(blk_iota == idx, neg, gate)
        self32 = sel.astype(F32)
        s_all = _dot_nt(qblk, knb[:(qi + 1) * BS]) * scale
        blocks = []
        for j in range(qi + 1):
            s = s_all[:, j * BS:(j + 1) * BS]
            if j == qi:
                s = jnp.where(causal, s + t0, neg)
            else:
                bias = t1 if j == qi - 1 else far
                picked = self32[:, j:j + 1] > 0.5
                s = jnp.where(picked, s + bias, neg)
            blocks.append(s)
        m = jnp.max(blocks[-1], axis=1, keepdims=True)
        for s in blocks[:-1]:
            m = jnp.maximum(m, jnp.max(s, axis=1, keepdims=True))
        l = jnp.zeros((BS, 1), F32)
        acc = jnp.zeros((BS, A_HEAD_DIM), F32)
        for j, s in enumerate(blocks):
            p = jnp.exp(s - m)
            l = l + jnp.sum(p, axis=1, keepdims=True)
            acc = acc + _dot(p, vb[j * BS:(j + 1) * BS])
        o_ref[qi * BS:(qi + 1) * BS, :] = (acc / l).astype(o_ref.dtype)


def _moba(proj, q_gain, k_gain, t0, t1, far, B, S):
    H = A_HEADS
    nb = S // MOBA_BLOCK
    dh = A_HEAD_DIM
    return pl.pallas_call(
        functools.partial(_moba_kernel, nb=nb),
        grid=(B, H),
        in_specs=[
            pl.BlockSpec((S, dh), lambda b, h: (b, h)),
            pl.BlockSpec((S, dh), lambda b, h: (b, H + h)),
            pl.BlockSpec((S, dh), lambda b, h: (b, 2 * H + h)),
            pl.BlockSpec((1, dh), lambda b, h: (0, 0)),
            pl.BlockSpec((1, dh), lambda b, h: (0, 0)),
            pl.BlockSpec((None, MOBA_BLOCK, MOBA_BLOCK), lambda b, h: (h, 0, 0)),
            pl.BlockSpec((None, MOBA_BLOCK, MOBA_BLOCK), lambda b, h: (h, 0, 0)),
            pl.BlockSpec((None, 8, MOBA_BLOCK), lambda b, h: (h, 0, 0)),
        ],
        out_specs=pl.BlockSpec((S, dh), lambda b, h: (b, h)),
        out_shape=jax.ShapeDtypeStruct((B * S, H * dh), BF16),
        compiler_params=_cparams(("parallel", "parallel")),
        name="moba",
    )(proj, proj, proj, q_gain, k_gain, t0, t1, far)


def _t5_bucket(rel):
    n = jnp.maximum(-rel, 0)
    max_exact = REL_BUCKETS // 2
    nf = jnp.maximum(n, 1).astype(F32)
    large = max_exact + (jnp.log(nf / max_exact) / math.log(REL_MAX_DIST / max_exact)
                         * (REL_BUCKETS - max_exact)).astype(jnp.int32)
    large = jnp.minimum(large, REL_BUCKETS - 1)
    return jnp.where(n < max_exact, n, large)


def _bias_tiles(rel_bias):
    assert MOBA_BLOCK + 1 >= REL_MAX_DIST
    offs = jnp.arange(MOBA_BLOCK)
    rel0 = offs[None, :] - offs[:, None]
    bias_hb = rel_bias.T.astype(F32)
    t0 = bias_hb[:, _t5_bucket(rel0)]
    t1 = bias_hb[:, _t5_bucket(rel0 - MOBA_BLOCK)]
    far = jnp.broadcast_to(bias_hb[:, REL_BUCKETS - 1][:, None, None], (A_HEADS, 8, MOBA_BLOCK))
    return t0, t1, far


def _gdn_kernel(q_ref, k_ref, v_ref, z_ref, sm_ref, cwq_ref, cwk_ref, cwv_ref, alog_ref, dtb_ref,
                ng_ref, o_ref, qs, ks, vs, bs, gs, *, S):
    C = DELTA_CHUNK
    G = GDN_GROUP
    dk = B_KEY_DIM
    hh = pl.program_id(1)
    row = lax.broadcasted_iota(jnp.int32, (S, dk), 0)

    def conv_silu(x, w):
        y = x * w[CONV_WIDTH - 1:CONV_WIDTH, :]
        for d in range(1, CONV_WIDTH):
            xs = jnp.where(row >= d, pltpu.roll(x, d, 0), 0.0)
            y = y + xs * w[CONV_WIDTH - 1 - d:CONV_WIDTH - d, :]
        return _silu(y)

    def l2n(x):
        return x * lax.rsqrt(jnp.sum(x * x, axis=-1, keepdims=True) + EPS)

    qs[...] = l2n(conv_silu(q_ref[...], cwq_ref[...])) * (dk ** -0.5)
    ks[...] = l2n(conv_silu(k_ref[...], cwk_ref[...]))
    vs[...] = conv_silu(v_ref[...], cwv_ref[...])

    sm = sm_ref[...]
    beta_all = _sigmoid(sm)
    z = sm + dtb_ref[...]
    softplus = jnp.maximum(z, 0.0) + jnp.log(1.0 + jnp.exp(-jnp.abs(z)))
    g_all = -jnp.exp(alog_ref[...]) * softplus
    sel_r = lax.broadcasted_iota(jnp.int32, (LANES, LANES), 0)
    bs[...] = _dot_hi(beta_all, (sel_r == hh).astype(F32))
    gs[...] = _dot_hi(g_all, (sel_r == hh + B_HEADS).astype(F32))

    ri = lax.broadcasted_iota(jnp.int32, (G, G), 0)
    ci = lax.broadcasted_iota(jnp.int32, (G, G), 1)
    same = (ri // C) == (ci // C)
    strict = same & (ri > ci)
    causal = same & (ri >= ci)
    tri = causal.astype(F32)
    strict_f = strict.astype(F32)
    eye = (ri == ci).astype(F32)
    ng = ng_ref[...]

    def group(gi, state):
        r0 = pl.multiple_of(gi * G, G)
        qg = qs[pl.ds(r0, G), :]
        kg = ks[pl.ds(r0, G), :]
        vg = vs[pl.ds(r0, G), :]
        bg = bs[pl.ds(r0, G), :]
        gg = gs[pl.ds(r0, G), :]
        gmat = jnp.concatenate([gg, gg], axis=1)
        dmat = _dot_hi(tri, gmat * strict_f)
        gcum = _dot_hi(tri, gg)
        decay = jnp.where(causal, jnp.exp(jnp.where(causal, dmat, 0.0)), 0.0)
        kb = kg * bg
        vb = vg * bg
        kk = _dot_nt_hi(kb, kg)
        a = -(kk * decay) * strict_f
        t = eye + a
        p = a
        for _ in range(int(math.log2(C)) - 1):
            p = _dot_hi(p, p)
            t = t + _dot_hi(t, p)
        eg = jnp.exp(gcum)
        u = _dot_hi(t, vb)
        w = _dot_hi(t, kb * eg)
        attn = _dot_nt_hi(qg, kg) * decay
        qd = qg * eg
        outs = []
        vnews = []
        for c in range(G // C):
            sl = slice(c * C, (c + 1) * C)
            glast = gcum[(c + 1) * C - 1:(c + 1) * C, :]
            kd = kg[sl] * jnp.exp(glast - gcum[sl])
            v_new = u[sl] - _dot_hi(w[sl], state)
            outs.append(_dot_hi(qd[sl], state))
            vnews.append(v_new)
            state = state * jnp.exp(glast)[:, 0:1] + lax.dot_general(
                kd, v_new, (((0,), (0,)), ((), ())), preferred_element_type=F32, precision=HIGHEST)
        o = jnp.concatenate(outs, axis=0) + _dot_hi(attn, jnp.concatenate(vnews, axis=0))
        on = o * lax.rsqrt(jnp.mean(o * o, axis=-1, keepdims=True) + EPS) * ng
        o_ref[pl.ds(r0, G), :] = (on * _silu(z_ref[pl.ds(r0, G), :])).astype(o_ref.dtype)
        return state

    lax.fori_loop(0, S // G, group, jnp.zeros((dk, B_VAL_DIM), F32))


def _gdn(proj, small, conv_w, alog_pad, dtb_pad, norm_gain, B, S):
    H = B_HEADS
    dk = B_KEY_DIM
    qoff = 3 * A_HEADS
    blk = lambda off: pl.BlockSpec((S, dk), lambda b, h: (b, off + h))
    cw = lambda off: pl.BlockSpec((CONV_WIDTH, dk), lambda b, h: (0, off + h))
    one = pl.BlockSpec((1, LANES), lambda b, h: (0, 0))
    return pl.pallas_call(
        functools.partial(_gdn_kernel, S=S),
        grid=(B, H),
        in_specs=[blk(qoff), blk(qoff + H), blk(qoff + 2 * H), blk(qoff + 3 * H),
                  pl.BlockSpec((S, LANES), lambda b, h: (b, 0)),
                  cw(0), cw(H), cw(2 * H), one, one, one],
        out_specs=pl.BlockSpec((S, dk), lambda b, h: (b, h)),
        out_shape=jax.ShapeDtypeStruct((B * S, H * B_VAL_DIM), BF16),
        scratch_shapes=[pltpu.VMEM((S, dk), F32)] * 5,
        compiler_params=_cparams(("parallel", "parallel")),
        name="gdn",
    )(proj, proj, proj, proj, small, conv_w, conv_w, conv_w, alog_pad, dtb_pad, norm_gain)


def _merge_kernel(oa_ref, ob_ref, wa_ref, wb_ref, ga_ref, gb_ref, o_ref):
    a = jnp.dot(oa_ref[...], wa_ref[...], preferred_element_type=F32)
    b = jnp.dot(ob_ref[...], wb_ref[...], preferred_element_type=F32)
    o_ref[...] = (_sigmoid(ga_ref[...]) * a + _sigmoid(gb_ref[...]) * b).astype(o_ref.dtype)


def _merge(o_a, o_b, w_up_a, w_up_b, proj, tm=512, tn=1024):
    T, K = o_a.shape
    D = w_up_a.shape[1]
    ga_off = (3 * A_HEADS + 4 * B_HEADS) * LANES // tn
    gb_off = ga_off + D // tn
    return pl.pallas_call(
        _merge_kernel,
        grid=(T // tm, D // tn),
        in_specs=[
            pl.BlockSpec((tm, K), lambda i, j: (i, 0)),
            pl.BlockSpec((tm, K), lambda i, j: (i, 0)),
            pl.BlockSpec((K, tn), lambda i, j: (0, j)),
            pl.BlockSpec((K, tn), lambda i, j: (0, j)),
            pl.BlockSpec((tm, tn), lambda i, j: (i, ga_off + j)),
            pl.BlockSpec((tm, tn), lambda i, j: (i, gb_off + j)),
        ],
        out_specs=pl.BlockSpec((tm, tn), lambda i, j: (i, j)),
        out_shape=jax.ShapeDtypeStruct((T, D), BF16),
        compiler_params=_cparams(("parallel", "parallel")),
        name="merge",
    )(o_a, o_b, w_up_a, w_up_b, proj, proj)


def _out_proj_kernel(m_ref, w_ref, x_ref, g_ref, y_ref, h_ref):
    y = x_ref[...] + jnp.dot(m_ref[...], w_ref[...], preferred_element_type=F32)
    y_ref[...] = y
    ms = jnp.mean(y * y, axis=-1, keepdims=True)
    h_ref[...] = (y * lax.rsqrt(ms + EPS) * g_ref[...]).astype(h_ref.dtype)


def _out_proj(merged, w_out, x2, gain2, tm=512):
    T, D = x2.shape
    return pl.pallas_call(
        _out_proj_kernel,
        grid=(T // tm,),
        in_specs=[
            pl.BlockSpec((tm, D), lambda i: (i, 0)),
            pl.BlockSpec((D, D), lambda i: (0, 0)),
            pl.BlockSpec((tm, D), lambda i: (i, 0)),
            pl.BlockSpec((1, D), lambda i: (0, 0)),
        ],
        out_specs=[pl.BlockSpec((tm, D), lambda i: (i, 0)), pl.BlockSpec((tm, D), lambda i: (i, 0))],
        out_shape=[jax.ShapeDtypeStruct((T, D), F32), jax.ShapeDtypeStruct((T, D), BF16)],
        compiler_params=_cparams(("parallel",)),
        name="out_proj",
    )(merged, w_out, x2, gain2)


def _matmul_kernel(a_ref, b_ref, o_ref):
    o_ref[...] = jnp.dot(a_ref[...], b_ref[...], preferred_element_type=F32)


def _peer_query(h2, w_query, tm=512):
    T, D = h2.shape
    N = w_query.shape[1]
    return pl.pallas_call(
        _matmul_kernel,
        grid=(T // tm,),
        in_specs=[pl.BlockSpec((tm, D), lambda i: (i, 0)), pl.BlockSpec((D, N), lambda i: (0, 0))],
        out_specs=pl.BlockSpec((tm, N), lambda i: (i, 0)),
        out_shape=jax.ShapeDtypeStruct((T, N), F32),
        compiler_params=_cparams(("parallel",)),
        name="peer_query",
    )(h2, w_query)


def _topk_rows(s, k, pos):
    n = s.shape[0]
    vals, idxs = [], []
    for _ in range(k):
        m = jnp.max(s, axis=0, keepdims=True)
        i = jnp.min(jnp.where(s == m, pos, n), axis=0, keepdims=True)
        vals.append(m)
        idxs.append(i)
        s = jnp.where(pos == i, -jnp.inf, s)
    return jnp.concatenate(vals, axis=0), jnp.concatenate(idxs, axis=0)


def _peer_route_kernel(q_ref, keys_ref, idx_ref, gate_ref, *, tt):
    K = PEER_TOPK
    half = PEER_QDIM // 2
    pos_k = lax.broadcasted_iota(jnp.int32, (PEER_NKEYS, tt), 0)
    pos_c = lax.broadcasted_iota(jnp.int32, (K * K, tt), 0)
    idx_rows, gate_rows = [], []
    for h in range(PEER_HEADS):
        tops = []
        for p in range(2):
            qh = q_ref[:, (2 * h + p) * half:(2 * h + p + 1) * half]
            sc = _dot_nt_hi(keys_ref[h, p], qh)
            tops.append(_topk_rows(sc, K, pos_k))
        (s0, i0), (s1, i1) = tops
        cand = jnp.concatenate([s0[i:i + 1] + s1 for i in range(K)], axis=0)
        cidx = jnp.concatenate([i0[i:i + 1] * PEER_NKEYS + i1 for i in range(K)], axis=0)
        best, bpos = _topk_rows(cand, K, pos_c)
        eidx = jnp.concatenate(
            [jnp.min(jnp.where(pos_c == bpos[r:r + 1], cidx, PEER_NKEYS * PEER_NKEYS),
                     axis=0, keepdims=True) for r in range(K)], axis=0)
        e = jnp.exp(best - best[0:1])
        g = e / jnp.sum(e, axis=0, keepdims=True)
        zero = jnp.zeros_like(g[0:1])
        for r in range(K):
            gate_rows += [zero, g[r:r + 1]]
        idx_rows.append(eidx)
    idx_ref[...] = jnp.concatenate(idx_rows, axis=0).T
    gate_ref[...] = jnp.concatenate(gate_rows, axis=0).T


def _peer_route(qry, sub_keys, tt=256):
    T, N = qry.shape
    PK = PEER_HEADS * PEER_TOPK
    return pl.pallas_call(
        functools.partial(_peer_route_kernel, tt=tt),
        grid=(T // tt,),
        in_specs=[pl.BlockSpec((tt, N), lambda i: (i, 0)),
                  pl.BlockSpec(sub_keys.shape, lambda i: (0, 0, 0, 0))],
        out_specs=[pl.BlockSpec((tt, PK), lambda i: (i, 0)), pl.BlockSpec((tt, 2 * PK), lambda i: (i, 0))],
        out_shape=[jax.ShapeDtypeStruct((T, PK), jnp.int32), jax.ShapeDtypeStruct((T, 2 * PK), F32)],
        compiler_params=_cparams(("parallel",)),
        name="peer_route",
    )(qry, sub_keys)


PEER_TOKENS_PER_STEP = 4
PEER_SLOTS = 2 * PEER_TOKENS_PER_STEP
PEER_TOKEN_GROUP = 16


def _gelu(x):
    return 0.5 * x * (1.0 + lax.erf(x * (2.0 ** -0.5)))


def _pack_expert_table(u, v):
    def pack(x):
        E, D = x.shape
        b = lax.bitcast_convert_type(x.astype(BF16), jnp.uint16).astype(jnp.uint32)
        return (b[:, :D // 2] | (b[:, D // 2:] << 16)).reshape(E, D // (2 * LANES), LANES)
    return jnp.concatenate([pack(u), pack(v)], axis=1)


def _peer_expert_kernel(idx_ref, gate_ref, h_ref, y_ref, tab_ref, o_ref, buf, sem, *, tt, D):
    PK = PEER_HEADS * PEER_TOPK
    GT = PEER_TOKEN_GROUP
    HD = D // 2
    R = HD // LANES

    def issue(t, slot, k0=0, k1=PK):
        for k in range(k0, k1):
            e = idx_ref[t, k]
            pltpu.make_async_copy(tab_ref.at[e], buf.at[slot, :, k], sem.at[slot]).start()

    def wait(slot):
        pltpu.make_async_copy(buf.at[slot], buf.at[slot], sem.at[slot]).wait()

    TPS = PEER_TOKENS_PER_STEP

    for t in range(TPS):
        issue(t, t)

    even = lax.broadcasted_iota(jnp.int32, (1, 2 * PK), 1) % 2 == 0

    def token(j, slot, hr, gg, prefetch):
        z = jnp.zeros((1, 2 * PK), F32)
        for r in range(R):
            ub = pltpu.bitcast(buf[slot, r], BF16)
            res = lax.dot_general(hr[r], ub, (((1,), (1,)), ((), ())), preferred_element_type=F32)
            z = z + jnp.where(even, res[j:j + 1], res[GT + j:GT + j + 1])
            prefetch(r)
        s = z + pltpu.roll(z, 1, 1)
        a = _gelu(s) * gg[j:j + 1]
        a2 = jnp.concatenate([pltpu.roll(a, 2 * PK - 1, 1), a], axis=0).astype(BF16)
        o2 = []
        for r in range(R):
            o2.append(jnp.dot(a2, pltpu.bitcast(buf[slot, R + r], BF16), preferred_element_type=F32))
            prefetch(R + r)
        return jnp.concatenate([o[0:1] for o in o2] + [o[1:2] for o in o2], axis=1)

    def group(t0, last):
        hg = h_ref[pl.ds(t0, GT), :]
        gg = gate_ref[pl.ds(t0, GT), :]
        hr = [jnp.concatenate([hg[:, r * LANES:(r + 1) * LANES],
                               hg[:, HD + r * LANES:HD + (r + 1) * LANES]], axis=0) for r in range(R)]
        outs = []
        for step in range(GT // TPS):
            bank, nbank = (step % 2) * TPS, ((step + 1) % 2) * TPS
            for i in range(TPS):
                wait(bank + i)
            for i in range(TPS):
                def prefetch(stage, i=i, step=step, nbank=nbank):
                    if not (last and step == GT // TPS - 1):
                        per = PK // (2 * R)
                        issue(t0 + (step + 1) * TPS + i, nbank + i, stage * per, (stage + 1) * per)
                outs.append(token(step * TPS + i, bank + i, hr, gg, prefetch))
        o_ref[pl.ds(t0, GT), :] = y_ref[pl.ds(t0, GT), :] + jnp.concatenate(outs, axis=0)

    def body(gi, c):
        group(pl.multiple_of(gi * GT, GT), False)
        return c

    lax.fori_loop(0, tt // GT - 1, body, 0)
    group(tt - GT, True)


def _peer_experts(idx, gate2, h2, y1, table, tt=256):
    T, D = y1.shape
    PK = PEER_HEADS * PEER_TOPK
    return pl.pallas_call(
        functools.partial(_peer_expert_kernel, tt=tt, D=D),
        grid=(T // tt,),
        in_specs=[
            pl.BlockSpec((tt, PK), lambda i: (i, 0), memory_space=pltpu.SMEM),
            pl.BlockSpec((tt, 2 * PK), lambda i: (i, 0)),
            pl.BlockSpec((tt, D), lambda i: (i, 0)),
            pl.BlockSpec((tt, D), lambda i: (i, 0)),
            pl.BlockSpec(memory_space=pl.ANY),
        ],
        out_specs=pl.BlockSpec((tt, D), lambda i: (i, 0)),
        out_shape=jax.ShapeDtypeStruct((T, D), F32),
        scratch_shapes=[pltpu.VMEM((PEER_SLOTS, table.shape[1], PK, LANES), jnp.uint32),
                        pltpu.SemaphoreType.DMA((PEER_SLOTS,))],
        compiler_params=pltpu.CompilerParams(dimension_semantics=("arbitrary",),
                                             vmem_limit_bytes=VMEM_LIMIT, disable_bounds_checks=True),
        name="peer_experts",
    )(idx, gate2, h2, y1, table)


def _layer(x, rel_bias, norm1_gain, w_in, conv_w, a_log, dt_bias, q_norm_gain, k_norm_gain,
           gdn_norm_gain, w_up_a, w_up_b, w_out, norm2_gain, peer_w_query, peer_sub_keys,
           peer_u, peer_v):
    B, S, D = x.shape
    T = B * S
    x2 = x.reshape(T, D)
    n_main = 3 * A_HEADS * A_HEAD_DIM + 4 * B_HEADS * B_KEY_DIM
    n_small = 2 * B_HEADS
    w_main = jnp.concatenate([w_in[:, :n_main], w_in[:, n_main + n_small:]], axis=1).astype(BF16)
    w_small = jnp.pad(w_in[:, n_main:n_main + n_small], ((0, 0), (0, LANES - n_small))).astype(BF16)
    proj, small = _in_proj(x2, norm1_gain[None, :], w_main, w_small)

    t0, t1, far = _bias_tiles(rel_bias)
    o_a = _moba(proj, q_norm_gain[None, :], k_norm_gain[None, :], t0, t1, far, B, S)

    pad8 = lambda v: jnp.pad(v, (B_HEADS, LANES - 2 * B_HEADS))[None, :]
    o_b = _gdn(proj, small, conv_w, pad8(a_log), pad8(dt_bias), gdn_norm_gain[None, :], B, S)

    merged = _merge(o_a, o_b, w_up_a.astype(BF16), w_up_b.astype(BF16), proj)
    y1, h2 = _out_proj(merged, w_out.astype(BF16), x2, norm2_gain[None, :])

    qry = _peer_query(h2, peer_w_query.astype(BF16))
    idx, gate2 = _peer_route(qry, peer_sub_keys)
    y = _peer_experts(idx, gate2, h2, y1, _pack_expert_table(peer_u, peer_v))
    return y.reshape(B, S, D)


def kernel(x, rel_bias, norm1_gain, w_in, conv_w, a_log, dt_bias, q_norm_gain, k_norm_gain,
           gdn_norm_gain, w_up_a, w_up_b, w_out, norm2_gain, peer_w_query, peer_sub_keys,
           peer_u, peer_v):
    depth = w_in.shape[0]
    for l in range(depth):
        x = _layer(x, rel_bias, norm1_gain[l], w_in[l], conv_w[l], a_log[l], dt_bias[l],
                   q_norm_gain[l], k_norm_gain[l], gdn_norm_gain[l], w_up_a[l], w_up_b[l],
                   w_out[l], norm2_gain[l], peer_w_query[l], peer_sub_keys[l], peer_u[l], peer_v[l])
    return x
```

```python
import functools
import math

import jax
import jax.numpy as jnp
import numpy as np
from jax import lax
from jax.experimental import pallas as pl
from jax.experimental.pallas import tpu as pltpu

F32 = jnp.float32
BF16 = jnp.bfloat16
HIGHEST = lax.Precision.HIGHEST

LANES = 128
EPS = 1e-6

A_HEADS = 8
A_HEAD_DIM = 128
MOBA_BLOCK = 256
MOBA_TOPK = 3
REL_BUCKETS = 32
REL_MAX_DIST = 128
B_HEADS = 8
B_KEY_DIM = 128
B_VAL_DIM = 128
CONV_WIDTH = 4
DELTA_CHUNK = 64
GDN_GROUP = 256
PEER_HEADS = 8
PEER_NKEYS = 128
PEER_QDIM = 256
PEER_TOPK = 16

VMEM_LIMIT = 56 * 1024 * 1024


def _cparams(sem):
    return pltpu.CompilerParams(dimension_semantics=sem, vmem_limit_bytes=VMEM_LIMIT)


def _dot(a, b):
    return jnp.dot(a.astype(BF16), b.astype(BF16), preferred_element_type=F32)


def _dot_nt(a, b):
    return lax.dot_general(a.astype(BF16), b.astype(BF16), (((1,), (1,)), ((), ())),
                           preferred_element_type=F32)


def _dot_tn(a, b):
    return lax.dot_general(a.astype(BF16), b.astype(BF16), (((0,), (0,)), ((), ())),
                           preferred_element_type=F32)


def _dot_hi(a, b):
    return jnp.dot(a, b, preferred_element_type=F32, precision=HIGHEST)


def _dot_nt_hi(a, b):
    return lax.dot_general(a, b, (((1,), (1,)), ((), ())), preferred_element_type=F32,
                           precision=HIGHEST)


def _split2(a):
    hi = a.astype(BF16)
    return hi, (a - hi.astype(F32)).astype(BF16)


def _mm(a, b):
    return jnp.dot(a, b, preferred_element_type=F32)


def _dot3(a2, b2):
    (ah, al), (bh, bl) = a2, b2
    return _mm(ah, bh) + (_mm(ah, bl) + _mm(al, bh))


def _dot_exact01(m01, x):
    m = m01.astype(BF16)
    hi = x.astype(BF16)
    r1 = x - hi.astype(F32)
    mid = r1.astype(BF16)
    lo = (r1 - mid.astype(F32)).astype(BF16)
    return _mm(m, hi) + (_mm(m, mid) + _mm(m, lo))


def _dot_exact01_rhs(x, m01):
    m = m01.astype(BF16)
    hi = x.astype(BF16)
    r1 = x - hi.astype(F32)
    mid = r1.astype(BF16)
    lo = (r1 - mid.astype(F32)).astype(BF16)
    return _mm(hi, m) + (_mm(mid, m) + _mm(lo, m))


def _sigmoid(x):
    return 1.0 / (1.0 + jnp.exp(-x))


def _silu(x):
    return x * _sigmoid(x)


def _in_proj_kernel(x_ref, g_ref, w_ref, ws_ref, o_ref, os_ref, h_scr):
    @pl.when(pl.program_id(1) == 0)
    def _():
        x = x_ref[...]
        ms = jnp.mean(x * x, axis=-1, keepdims=True)
        h = (x * lax.rsqrt(ms + EPS) * g_ref[...]).astype(BF16)
        h_scr[...] = h
        os_ref[...] = jnp.dot(h, ws_ref[...], preferred_element_type=F32)

    o_ref[...] = jnp.dot(h_scr[...], w_ref[...], preferred_element_type=F32)


def _in_proj(x2, gain, w_main, w_small, tm=512, tn=1024):
    T, D = x2.shape
    N = w_main.shape[1]
    return pl.pallas_call(
        _in_proj_kernel,
        grid=(T // tm, N // tn),
        in_specs=[
            pl.BlockSpec((tm, D), lambda i, j: (i, 0)),
            pl.BlockSpec((1, D), lambda i, j: (0, 0)),
            pl.BlockSpec((D, tn), lambda i, j: (0, j)),
            pl.BlockSpec((D, LANES), lambda i, j: (0, 0)),
        ],
        out_specs=[
            pl.BlockSpec((tm, tn), lambda i, j: (i, j)),
            pl.BlockSpec((tm, LANES), lambda i, j: (i, 0)),
        ],
        out_shape=[jax.ShapeDtypeStruct((T, N), F32), jax.ShapeDtypeStruct((T, LANES), F32)],
        scratch_shapes=[pltpu.VMEM((tm, D), BF16)],
        compiler_params=_cparams(("parallel", "arbitrary")),
        name="in_proj",
    )(x2, gain, w_main, w_small)


def _moba_kernel(q_ref, k_ref, v_ref, qg_ref, kg_ref, t0_ref, t1_ref, far_ref, o_ref, *, nb):
    BS = MOBA_BLOCK
    scale = A_HEAD_DIM ** -0.5
    q = q_ref[...]
    k = k_ref[...]
    qn = q * lax.rsqrt(jnp.mean(q * q, axis=-1, keepdims=True) + EPS) * qg_ref[...]
    kn = k * lax.rsqrt(jnp.mean(k * k, axis=-1, keepdims=True) + EPS) * kg_ref[...]
    kmean = jnp.concatenate(
        [jnp.mean(kn[j * BS:(j + 1) * BS], axis=0, keepdims=True) for j in range(nb)], axis=0)
    knb = kn.astype(BF16)
    vb = v_ref[...].astype(BF16)
    t0 = t0_ref[...]
    t1 = t1_ref[...]
    far = far_ref[0:1, :]
    rows = lax.broadcasted_iota(jnp.int32, (BS, BS), 0)
    cols = lax.broadcasted_iota(jnp.int32, (BS, BS), 1)
    causal = cols <= rows
    blk_iota = lax.broadcasted_iota(jnp.int32, (BS, nb), 1)
    neg = -jnp.inf
    for qi in range(nb):
        qblk = qn[qi * BS:(qi + 1) * BS]
        sel = jnp.zeros((BS, nb), dtype=jnp.bool_)
        if qi > 0:
            gate = _dot_nt_hi(qblk, kmean)
            gate = jnp.where(blk_iota < qi, gate, neg)
            for _ in range(min(MOBA_TOPK, nb)):
                m = jnp.max(gate, axis=1, keepdims=True)
                idx = jnp.min(jnp.where(gate == m, blk_iota, nb), axis=1, keepdims=True)
                hit = (blk_iota == idx) & (m > neg)
                sel = sel | hit
                gate = jnp.where(blk_iota == idx, neg, gate)
        self32 = sel.astype(F32)
        s_all = _dot_nt(qblk, knb[:(qi + 1) * BS]) * scale
        blocks = []
        for j in range(qi + 1):
            s = s_all[:, j * BS:(j + 1) * BS]
            if j == qi:
                s = jnp.where(causal, s + t0, neg)
            else:
                bias = t1 if j == qi - 1 else far
                picked = self32[:, j:j + 1] > 0.5
                s = jnp.where(picked, s + bias, neg)
            blocks.append(s)
        m = jnp.max(blocks[-1], axis=1, keepdims=True)
        for s in blocks[:-1]:
            m = jnp.maximum(m, jnp.max(s, axis=1, keepdims=True))
        l = jnp.zeros((BS, 1), F32)
        acc = jnp.zeros((BS, A_HEAD_DIM), F32)
        for j, s in enumerate(blocks):
            p = jnp.exp(s - m)
            l = l + jnp.sum(p, axis=1, keepdims=True)
            acc = acc + _dot(p, vb[j * BS:(j + 1) * BS])
        o_ref[qi * BS:(qi + 1) * BS, :] = (acc / l).astype(o_ref.dtype)


def _moba(proj, q_gain, k_gain, t0, t1, far, B, S):
    H = A_HEADS
    nb = S // MOBA_BLOCK
    dh = A_HEAD_DIM
    return pl.pallas_call(
        functools.partial(_moba_kernel, nb=nb),
        grid=(B, H),
        in_specs=[
            pl.BlockSpec((S, dh), lambda b, h: (b, h)),
            pl.BlockSpec((S, dh), lambda b, h: (b, H + h)),
            pl.BlockSpec((S, dh), lambda b, h: (b, 2 * H + h)),
            pl.BlockSpec((1, dh), lambda b, h: (0, 0)),
            pl.BlockSpec((1, dh), lambda b, h: (0, 0)),
            pl.BlockSpec((None, MOBA_BLOCK, MOBA_BLOCK), lambda b, h: (h, 0, 0)),
            pl.BlockSpec((None, MOBA_BLOCK, MOBA_BLOCK), lambda b, h: (h, 0, 0)),
            pl.BlockSpec((None, 8, MOBA_BLOCK), lambda b, h: (h, 0, 0)),
        ],
        out_specs=pl.BlockSpec((S, dh), lambda b, h: (b, h)),
        out_shape=jax.ShapeDtypeStruct((B * S, H * dh), BF16),
        compiler_params=_cparams(("parallel", "parallel")),
        name="moba",
    )(proj, proj, proj, q_gain, k_gain, t0, t1, far)


def _t5_bucket(rel):
    n = jnp.maximum(-rel, 0)
    max_exact = REL_BUCKETS // 2
    nf = jnp.maximum(n, 1).astype(F32)
    large = max_exact + (jnp.log(nf / max_exact) / math.log(REL_MAX_DIST / max_exact)
                         * (REL_BUCKETS - max_exact)).astype(jnp.int32)
    large = jnp.minimum(large, REL_BUCKETS - 1)
    return jnp.where(n < max_exact, n, large)


def _bias_tiles(rel_bias):
    assert MOBA_BLOCK + 1 >= REL_MAX_DIST
    offs = jnp.arange(MOBA_BLOCK)
    rel0 = offs[None, :] - offs[:, None]
    bias_hb = rel_bias.T.astype(F32)

    def tile(rel):
        onehot = jax.nn.one_hot(_t5_bucket(rel), REL_BUCKETS, dtype=F32)
        return jnp.einsum('hb,qkb->hqk', bias_hb, onehot, precision=HIGHEST)

    t0 = tile(rel0)
    t1 = tile(rel0 - MOBA_BLOCK)
    far = jnp.broadcast_to(bias_hb[:, REL_BUCKETS - 1][:, None, None], (A_HEADS, 8, MOBA_BLOCK))
    return t0, t1, far


def _gdn_kernel(q_ref, k_ref, v_ref, z_ref, sm_ref, cwq_ref, cwk_ref, cwv_ref, alog_ref, dtb_ref,
                ng_ref, o_ref, qs, ks, vs, bs, gs, *, S):
    C = DELTA_CHUNK
    G = GDN_GROUP
    dk = B_KEY_DIM
    hh = pl.program_id(1)
    row = lax.broadcasted_iota(jnp.int32, (S, dk), 0)

    def conv_silu(x, w):
        y = x * w[CONV_WIDTH - 1:CONV_WIDTH, :]
        for d in range(1, CONV_WIDTH):
            xs = jnp.where(row >= d, pltpu.roll(x, d, 0), 0.0)
            y = y + xs * w[CONV_WIDTH - 1 - d:CONV_WIDTH - d, :]
        return _silu(y)

    def l2n(x):
        return x * lax.rsqrt(jnp.sum(x * x, axis=-1, keepdims=True) + EPS)

    qs[...] = l2n(conv_silu(q_ref[...], cwq_ref[...])) * (dk ** -0.5)
    ks[...] = l2n(conv_silu(k_ref[...], cwk_ref[...]))
    vs[...] = conv_silu(v_ref[...], cwv_ref[...])

    sm = sm_ref[...]
    beta_all = _sigmoid(sm)
    z = sm + dtb_ref[...]
    softplus = jnp.maximum(z, 0.0) + jnp.log(1.0 + jnp.exp(-jnp.abs(z)))
    g_all = -jnp.exp(alog_ref[...]) * softplus
    sel_r = lax.broadcasted_iota(jnp.int32, (LANES, LANES), 0)
    bs[...] = _dot_exact01_rhs(beta_all, sel_r == hh)
    gs[...] = _dot_exact01_rhs(g_all, sel_r == hh + B_HEADS)

    ri = lax.broadcasted_iota(jnp.int32, (G, G), 0)
    ci = lax.broadcasted_iota(jnp.int32, (G, G), 1)
    same = (ri // C) == (ci // C)
    strict = same & (ri > ci)
    causal = same & (ri >= ci)
    tri = causal
    strict_f = strict.astype(F32)
    eye = (ri == ci).astype(F32)
    ng = ng_ref[...]

    def group(gi, state):
        r0 = pl.multiple_of(gi * G, G)
        qg = qs[pl.ds(r0, G), :]
        kg = ks[pl.ds(r0, G), :]
        vg = vs[pl.ds(r0, G), :]
        bg = bs[pl.ds(r0, G), :]
        gg = gs[pl.ds(r0, G), :]
        gmat = jnp.concatenate([gg, gg], axis=1)
        dmat = _dot_exact01(tri, gmat * strict_f)
        gcum = _dot_exact01(tri, gg)
        decay = jnp.where(causal, jnp.exp(jnp.where(causal, dmat, 0.0)), 0.0)
        kb = kg * bg
        vb = vg * bg
        kk = _dot_nt(kb, kg)
        a = -(kk * decay) * strict_f
        t = eye + a
        p2 = _split2(a)
        for _ in range(int(math.log2(C)) - 1):
            p2 = _split2(_dot3(p2, p2))
            t = t + _dot3(_split2(t), p2)
        eg = jnp.exp(gcum)
        tb = t.astype(BF16)
        u = _mm(tb, vb.astype(BF16))
        w = _mm(tb, (kb * eg).astype(BF16))
        attn = _dot_nt(qg, kg) * decay
        qd = qg * eg
        outs = []
        vnews = []
        for c in range(G // C):
            sl = slice(c * C, (c + 1) * C)
            glast = gcum[(c + 1) * C - 1:(c + 1) * C, :]
            kd = kg[sl] * jnp.exp(glast - gcum[sl])
            sb = state.astype(BF16)
            v_new = u[sl] - _mm(w[sl].astype(BF16), sb)
            outs.append(_mm(qd[sl].astype(BF16), sb))
            vnews.append(v_new)
            state = state * jnp.exp(glast)[:, 0:1] + _dot_tn(kd, v_new)
        o = jnp.concatenate(outs, axis=0) + _dot(attn, jnp.concatenate(vnews, axis=0))
        on = o * lax.rsqrt(jnp.mean(o * o, axis=-1, keepdims=True) + EPS) * ng
        o_ref[pl.ds(r0, G), :] = (on * _silu(z_ref[pl.ds(r0, G), :])).astype(o_ref.dtype)
        return state

    lax.fori_loop(0, S // G, group, jnp.zeros((dk, B_VAL_DIM), F32))


def _gdn(proj, small, conv_w, alog_pad, dtb_pad, norm_gain, B, S):
    H = B_HEADS
    dk = B_KEY_DIM
    qoff = 3 * A_HEADS
    blk = lambda off: pl.BlockSpec((S, dk), lambda b, h: (b, off + h))
    cw = lambda off: pl.BlockSpec((CONV_WIDTH, dk), lambda b, h: (0, off + h))
    one = pl.BlockSpec((1, LANES), lambda b, h: (0, 0))
    return pl.pallas_call(
        functools.partial(_gdn_kernel, S=S),
        grid=(B, H),
        in_specs=[blk(qoff), blk(qoff + H), blk(qoff + 2 * H), blk(qoff + 3 * H),
                  pl.BlockSpec((S, LANES), lambda b, h: (b, 0)),
                  cw(0), cw(H), cw(2 * H), one, one, one],
        out_specs=pl.BlockSpec((S, dk), lambda b, h: (b, h)),
        out_shape=jax.ShapeDtypeStruct((B * S, H * B_VAL_DIM), BF16),
        scratch_shapes=[pltpu.VMEM((S, dk), F32)] * 5,
        compiler_params=_cparams(("parallel", "parallel")),
        name="gdn",
    )(proj, proj, proj, proj, small, conv_w, conv_w, conv_w, alog_pad, dtb_pad, norm_gain)


def _merge_kernel(oa_ref, ob_ref, wa_ref, wb_ref, ga_ref, gb_ref, o_ref):
    a = jnp.dot(oa_ref[...], wa_ref[...], preferred_element_type=F32)
    b = jnp.dot(ob_ref[...], wb_ref[...], preferred_element_type=F32)
    o_ref[...] = (_sigmoid(ga_ref[...]) * a + _sigmoid(gb_ref[...]) * b).astype(o_ref.dtype)


def _merge(o_a, o_b, w_up_a, w_up_b, proj, tm=512, tn=1024):
    T, K = o_a.shape
    D = w_up_a.shape[1]
    ga_off = (3 * A_HEADS + 4 * B_HEADS) * LANES // tn
    gb_off = ga_off + D // tn
    return pl.pallas_call(
        _merge_kernel,
        grid=(T // tm, D // tn),
        in_specs=[
            pl.BlockSpec((tm, K), lambda i, j: (i, 0)),
            pl.BlockSpec((tm, K), lambda i, j: (i, 0)),
            pl.BlockSpec((K, tn), lambda i, j: (0, j)),
            pl.BlockSpec((K, tn), lambda i, j: (0, j)),
            pl.BlockSpec((tm, tn), lambda i, j: (i, ga_off + j)),
            pl.BlockSpec((tm, tn), lambda i, j: (i, gb_off + j)),
        ],
        out_specs=pl.BlockSpec((tm, tn), lambda i, j: (i, j)),
        out_shape=jax.ShapeDtypeStruct((T, D), BF16),
        compiler_params=_cparams(("parallel", "parallel")),
        name="merge",
    )(o_a, o_b, w_up_a, w_up_b, proj, proj)


def _out_proj_kernel(m_ref, w_ref, x_ref, g_ref, y_ref, h_ref):
    y = x_ref[...] + jnp.dot(m_ref[...], w_ref[...], preferred_element_type=F32)
    y_ref[...] = y
    ms = jnp.mean(y * y, axis=-1, keepdims=True)
    h_ref[...] = (y * lax.rsqrt(ms + EPS) * g_ref[...]).astype(h_ref.dtype)


def _out_proj(merged, w_out, x2, gain2, tm=512):
    T, D = x2.shape
    return pl.pallas_call(
        _out_proj_kernel,
        grid=(T // tm,),
        in_specs=[
            pl.BlockSpec((tm, D), lambda i: (i, 0)),
            pl.BlockSpec((D, D), lambda i: (0, 0)),
            pl.BlockSpec((tm, D), lambda i: (i, 0)),
            pl.BlockSpec((1, D), lambda i: (0, 0)),
        ],
        out_specs=[pl.BlockSpec((tm, D), lambda i: (i, 0)), pl.BlockSpec((tm, D), lambda i: (i, 0))],
        out_shape=[jax.ShapeDtypeStruct((T, D), F32), jax.ShapeDtypeStruct((T, D), BF16)],
        compiler_params=_cparams(("parallel",)),
        name="out_proj",
    )(merged, w_out, x2, gain2)


def _matmul_kernel(a_ref, b_ref, o_ref):
    o_ref[...] = jnp.dot(a_ref[...], b_ref[...], preferred_element_type=F32)


def _peer_query(h2, w_query, tm=512):
    T, D = h2.shape
    N = w_query.shape[1]
    return pl.pallas_call(
        _matmul_kernel,
        grid=(T // tm,),
        in_specs=[pl.BlockSpec((tm, D), lambda i: (i, 0)), pl.BlockSpec((D, N), lambda i: (0, 0))],
        out_specs=pl.BlockSpec((tm, N), lambda i: (i, 0)),
        out_shape=jax.ShapeDtypeStruct((T, N), F32),
        compiler_params=_cparams(("parallel",)),
        name="peer_query",
    )(h2, w_query)


_NO_POS = 2 ** 30


def _topk_rows(s, k, pos):
    vals, idxs = [], []
    for _ in range(k):
        m = jnp.max(s, axis=0, keepdims=True)
        i = jnp.min(jnp.where(s == m, pos, _NO_POS), axis=0, keepdims=True)
        vals.append(m)
        idxs.append(i)
        s = jnp.where(pos == i, -jnp.inf, s)
    return jnp.concatenate(vals, axis=0), jnp.concatenate(idxs, axis=0)


def _peer_route_kernel(q_ref, keys_ref, idx_ref, gate_ref, *, tt):
    K = PEER_TOPK
    half = PEER_QDIM // 2
    pos_k = lax.broadcasted_iota(jnp.int32, (PEER_NKEYS, tt), 0)
    row_k = lax.broadcasted_iota(jnp.int32, (K, tt), 0)
    row_8 = lax.broadcasted_iota(jnp.int32, (8, tt), 0)
    assert K == 16 and all((i < 4) or (i < 8 and j < 4) or (j == 0)
                           for i in range(K) for j in range(K) if (i + 1) * (j + 1) <= K)
    idx_rows, gate_rows = [], []
    for h in range(PEER_HEADS):
        tops = []
        for p in range(2):
            qh = q_ref[:, (2 * h + p) * half:(2 * h + p + 1) * half]
            sc = _dot_nt_hi(keys_ref[h, p], qh)
            tops.append(_topk_rows(sc, K, pos_k))
        (s0, i0), (s1, i1) = tops
        cb, pb, ib = [], [], []
        for i in range(4):
            cb.append(s0[i:i + 1] + s1)
            pb.append(i * K + row_k)
            ib.append(i0[i:i + 1] * PEER_NKEYS + i1)
        for j in range(4):
            cb.append(jnp.where(row_8 >= 4, s0[0:8] + s1[j:j + 1], -jnp.inf))
            pb.append(row_8 * K + j)
            ib.append(i0[0:8] * PEER_NKEYS + i1[j:j + 1])
        cb.append(s0[8:16] + s1[0:1])
        pb.append((row_8 + 8) * K)
        ib.append(i0[8:16] * PEER_NKEYS + i1[0:1])
        cand = jnp.concatenate(cb, axis=0)
        pos_c = jnp.concatenate(pb, axis=0)
        cidx = jnp.concatenate(ib, axis=0)
        best, bpos = _topk_rows(cand, K, pos_c)
        eidx = jnp.concatenate(
            [jnp.min(jnp.where(pos_c == bpos[r:r + 1], cidx, PEER_NKEYS * PEER_NKEYS),
                     axis=0, keepdims=True) for r in range(K)], axis=0)
        e = jnp.exp(best - best[0:1])
        g = e / jnp.sum(e, axis=0, keepdims=True)
        zero = jnp.zeros_like(g[0:1])
        for r in range(K):
            gate_rows += [zero, g[r:r + 1]]
        idx_rows.append(eidx)
    idx_ref[...] = jnp.concatenate(idx_rows, axis=0).T
    gate_ref[...] = jnp.concatenate(gate_rows, axis=0).T


def _peer_route(qry, sub_keys, tt=256):
    T, N = qry.shape
    PK = PEER_HEADS * PEER_TOPK
    return pl.pallas_call(
        functools.partial(_peer_route_kernel, tt=tt),
        grid=(T // tt,),
        in_specs=[pl.BlockSpec((tt, N), lambda i: (i, 0)),
                  pl.BlockSpec(sub_keys.shape, lambda i: (0, 0, 0, 0))],
        out_specs=[pl.BlockSpec((tt, PK), lambda i: (i, 0)), pl.BlockSpec((tt, 2 * PK), lambda i: (i, 0))],
        out_shape=[jax.ShapeDtypeStruct((T, PK), jnp.int32), jax.ShapeDtypeStruct((T, 2 * PK), F32)],
        compiler_params=_cparams(("parallel",)),
        name="peer_route",
    )(qry, sub_keys)


PEER_TOKENS_PER_STEP = 4
PEER_SLOTS = 2 * PEER_TOKENS_PER_STEP
PEER_TOKEN_GROUP = 16


def _gelu(x):
    return 0.5 * x * (1.0 + lax.erf(x * (2.0 ** -0.5)))


def _pack_expert_table(u, v):
    def pack(x):
        E, D = x.shape
        b = lax.bitcast_convert_type(x.astype(BF16), jnp.uint16).astype(jnp.uint32)
        return (b[:, :D // 2] | (b[:, D // 2:] << 16)).reshape(E, D // (2 * LANES), LANES)
    return jnp.concatenate([pack(u), pack(v)], axis=1)


def _peer_expert_kernel(idx_ref, gate_ref, h_ref, y_ref, tab_ref, o_ref, buf, sem, *, tt, D):
    PK = PEER_HEADS * PEER_TOPK
    GT = PEER_TOKEN_GROUP
    HD = D // 2
    R = HD // LANES

    def issue(t, slot, k0=0, k1=PK):
        for k in range(k0, k1):
            e = idx_ref[t, k]
            pltpu.make_async_copy(tab_ref.at[e], buf.at[slot, :, k], sem.at[slot]).start(priority=slot % 2)

    def wait(slot):
        pltpu.make_async_copy(buf.at[slot], buf.at[slot], sem.at[slot]).wait()

    TPS = PEER_TOKENS_PER_STEP

    for t in range(TPS):
        issue(t, t)

    even = lax.broadcasted_iota(jnp.int32, (1, 2 * PK), 1) % 2 == 0

    def token(j, slot, hr, gg, prefetch):
        z = jnp.zeros((1, 2 * PK), F32)
        for r in range(R):
            ub = pltpu.bitcast(buf[slot, r], BF16)
            res = lax.dot_general(hr[r], ub, (((1,), (1,)), ((), ())), preferred_element_type=F32)
            z = z + jnp.where(even, res[j:j + 1], res[GT + j:GT + j + 1])
            prefetch(r)
        s = z + pltpu.roll(z, 1, 1)
        a = _gelu(s) * gg[j:j + 1]
        a2 = jnp.concatenate([pltpu.roll(a, 2 * PK - 1, 1), a], axis=0).astype(BF16)
        o2 = []
        for r in range(R):
            o2.append(jnp.dot(a2, pltpu.bitcast(buf[slot, R + r], BF16), preferred_element_type=F32))
            prefetch(R + r)
        return jnp.concatenate([o[0:1] for o in o2] + [o[1:2] for o in o2], axis=1)

    def group(t0, last):
        hg = h_ref[pl.ds(t0, GT), :]
        gg = gate_ref[pl.ds(t0, GT), :]
        hr = [jnp.concatenate([hg[:, r * LANES:(r + 1) * LANES],
                               hg[:, HD + r * LANES:HD + (r + 1) * LANES]], axis=0) for r in range(R)]
        outs = []
        for step in range(GT // TPS):
            bank, nbank = (step % 2) * TPS, ((step + 1) % 2) * TPS
            for i in range(TPS):
                wait(bank + i)
            for i in range(TPS):
                def prefetch(stage, i=i, step=step, nbank=nbank):
                    if not (last and step == GT // TPS - 1):
                        per = PK // (2 * R)
                        issue(t0 + (step + 1) * TPS + i, nbank + i, stage * per, (stage + 1) * per)
                outs.append(token(step * TPS + i, bank + i, hr, gg, prefetch))
        o_ref[pl.ds(t0, GT), :] = y_ref[pl.ds(t0, GT), :] + jnp.concatenate(outs, axis=0)

    def body(gi, c):
        group(pl.multiple_of(gi * GT, GT), False)
        return c

    lax.fori_loop(0, tt // GT - 1, body, 0)
    group(tt - GT, True)


def _peer_experts(idx, gate2, h2, y1, table, tt=256):
    T, D = y1.shape
    PK = PEER_HEADS * PEER_TOPK
    return pl.pallas_call(
        functools.partial(_peer_expert_kernel, tt=tt, D=D),
        grid=(T // tt,),
        in_specs=[
            pl.BlockSpec((tt, PK), lambda i: (i, 0), memory_space=pltpu.SMEM),
            pl.BlockSpec((tt, 2 * PK), lambda i: (i, 0)),
            pl.BlockSpec((tt, D), lambda i: (i, 0)),
            pl.BlockSpec((tt, D), lambda i: (i, 0)),
            pl.BlockSpec(memory_space=pl.ANY),
        ],
        out_specs=pl.BlockSpec((tt, D), lambda i: (i, 0)),
        out_shape=jax.ShapeDtypeStruct((T, D), F32),
        scratch_shapes=[pltpu.VMEM((PEER_SLOTS, table.shape[1], PK, LANES), jnp.uint32),
                        pltpu.SemaphoreType.DMA((PEER_SLOTS,))],
        compiler_params=pltpu.CompilerParams(dimension_semantics=("arbitrary",),
                                             vmem_limit_bytes=VMEM_LIMIT, disable_bounds_checks=True),
        name="peer_experts",
    )(idx, gate2, h2, y1, table)


def _layer(x, rel_bias, norm1_gain, w_in, conv_w, a_log, dt_bias, q_norm_gain, k_norm_gain,
           gdn_norm_gain, w_up_a, w_up_b, w_out, norm2_gain, peer_w_query, peer_sub_keys,
           peer_u, peer_v):
    B, S, D = x.shape
    T = B * S
    x2 = x.reshape(T, D)
    n_main = 3 * A_HEADS * A_HEAD_DIM + 4 * B_HEADS * B_KEY_DIM
    n_small = 2 * B_HEADS
    w_main = jnp.concatenate([w_in[:, :n_main], w_in[:, n_main + n_small:]], axis=1).astype(BF16)
    w_small = jnp.pad(w_in[:, n_main:n_main + n_small], ((0, 0), (0, LANES - n_small))).astype(BF16)
    proj, small = _in_proj(x2, norm1_gain[None, :], w_main, w_small)

    t0, t1, far = _bias_tiles(rel_bias)
    o_a = _moba(proj, q_norm_gain[None, :], k_norm_gain[None, :], t0, t1, far, B, S)

    pad8 = lambda v: jnp.pad(v, (B_HEADS, LANES - 2 * B_HEADS))[None, :]
    o_b = _gdn(proj, small, conv_w, pad8(a_log), pad8(dt_bias), gdn_norm_gain[None, :], B, S)

    merged = _merge(o_a, o_b, w_up_a.astype(BF16), w_up_b.astype(BF16), proj)
    y1, h2 = _out_proj(merged, w_out.astype(BF16), x2, norm2_gain[None, :])

    qry = _peer_query(h2, peer_w_query.astype(BF16))
    idx, gate2 = _peer_route(qry, peer_sub_keys)
    y = _peer_experts(idx, gate2, h2, y1, _pack_expert_table(peer_u, peer_v))
    return y.reshape(B, S, D)


def kernel(x, rel_bias, norm1_gain, w_in, conv_w, a_log, dt_bias, q_norm_gain, k_norm_gain,
           gdn_norm_gain, w_up_a, w_up_b, w_out, norm2_gain, peer_w_query, peer_sub_keys,
           peer_u, peer_v):
    depth = w_in.shape[0]
    for l in range(depth):
        x = _layer(x, rel_bias, norm1_gain[l], w_in[l], conv_w[l], a_log[l], dt_bias[l],
                   q_norm_gain[l], k_norm_gain[l], gdn_norm_gain[l], w_up_a[l], w_up_b[l],
                   w_out[l], norm2_gain[l], peer_w_query[l], peer_sub_keys[l], peer_u[l], peer_v[l])
    return x
```

```python
import functools
import math

import jax
import jax.numpy as jnp
import numpy as np
from jax import lax
from jax.experimental import pallas as pl
from jax.experimental.pallas import tpu as pltpu

F32 = jnp.float32
BF16 = jnp.bfloat16
HIGHEST = lax.Precision.HIGHEST

LANES = 128
EPS = 1e-6

A_HEADS = 8
A_HEAD_DIM = 128
MOBA_BLOCK = 256
MOBA_TOPK = 3
REL_BUCKETS = 32
REL_MAX_DIST = 128
B_HEADS = 8
B_KEY_DIM = 128
B_VAL_DIM = 128
CONV_WIDTH = 4
DELTA_CHUNK = 64
GDN_GROUP = 256
PEER_HEADS = 8
PEER_NKEYS = 128
PEER_QDIM = 256
PEER_TOPK = 16

VMEM_LIMIT = 56 * 1024 * 1024


def _cparams(sem):
    return pltpu.CompilerParams(dimension_semantics=sem, vmem_limit_bytes=VMEM_LIMIT)


def _dot(a, b):
    return jnp.dot(a.astype(BF16), b.astype(BF16), preferred_element_type=F32)


def _dot_nt(a, b):
    return lax.dot_general(a.astype(BF16), b.astype(BF16), (((1,), (1,)), ((), ())),
                           preferred_element_type=F32)


def _dot_tn(a, b):
    return lax.dot_general(a.astype(BF16), b.astype(BF16), (((0,), (0,)), ((), ())),
                           preferred_element_type=F32)


def _dot_hi(a, b):
    return jnp.dot(a, b, preferred_element_type=F32, precision=HIGHEST)


def _dot_nt_hi(a, b):
    return lax.dot_general(a, b, (((1,), (1,)), ((), ())), preferred_element_type=F32,
                           precision=HIGHEST)


def _split2(a):
    hi = a.astype(BF16)
    return hi, (a - hi.astype(F32)).astype(BF16)


def _mm(a, b):
    return jnp.dot(a, b, preferred_element_type=F32)


def _dot3(a2, b2):
    (ah, al), (bh, bl) = a2, b2
    return _mm(ah, bh) + (_mm(ah, bl) + _mm(al, bh))


def _dot_exact01(m01, x):
    m = m01.astype(BF16)
    hi = x.astype(BF16)
    r1 = x - hi.astype(F32)
    mid = r1.astype(BF16)
    lo = (r1 - mid.astype(F32)).astype(BF16)
    return _mm(m, hi) + (_mm(m, mid) + _mm(m, lo))


def _dot_exact01_rhs(x, m01):
    m = m01.astype(BF16)
    hi = x.astype(BF16)
    r1 = x - hi.astype(F32)
    mid = r1.astype(BF16)
    lo = (r1 - mid.astype(F32)).astype(BF16)
    return _mm(hi, m) + (_mm(mid, m) + _mm(lo, m))


def _sigmoid(x):
    return 1.0 / (1.0 + jnp.exp(-x))


def _silu(x):
    return x * _sigmoid(x)


def _in_proj_kernel(x_ref, g_ref, w_ref, ws_ref, o_ref, os_ref, h_scr):
    @pl.when(pl.program_id(1) == 0)
    def _():
        x = x_ref[...]
        ms = jnp.mean(x * x, axis=-1, keepdims=True)
        h = (x * lax.rsqrt(ms + EPS) * g_ref[...]).astype(BF16)
        h_scr[...] = h
        os_ref[...] = jnp.dot(h, ws_ref[...], preferred_element_type=F32)

    o_ref[...] = jnp.dot(h_scr[...], w_ref[...], preferred_element_type=F32)


def _in_proj(x2, gain, w_main, w_small, tm=1024, tn=1024):
    T, D = x2.shape
    N = w_main.shape[1]
    return pl.pallas_call(
        _in_proj_kernel,
        grid=(T // tm, N // tn),
        in_specs=[
            pl.BlockSpec((tm, D), lambda i, j: (i, 0)),
            pl.BlockSpec((1, D), lambda i, j: (0, 0)),
            pl.BlockSpec((D, tn), lambda i, j: (0, j)),
            pl.BlockSpec((D, LANES), lambda i, j: (0, 0)),
        ],
        out_specs=[
            pl.BlockSpec((tm, tn), lambda i, j: (i, j)),
            pl.BlockSpec((tm, LANES), lambda i, j: (i, 0)),
        ],
        out_shape=[jax.ShapeDtypeStruct((T, N), F32), jax.ShapeDtypeStruct((T, LANES), F32)],
        scratch_shapes=[pltpu.VMEM((tm, D), BF16)],
        compiler_params=_cparams(("parallel", "arbitrary")),
        name="in_proj",
    )(x2, gain, w_main, w_small)


def _moba_kernel(q_ref, k_ref, v_ref, qg_ref, kg_ref, t0_ref, t1_ref, far_ref, o_ref, *, nb):
    BS = MOBA_BLOCK
    scale = A_HEAD_DIM ** -0.5
    q = q_ref[...]
    k = k_ref[...]
    qn = q * lax.rsqrt(jnp.mean(q * q, axis=-1, keepdims=True) + EPS) * qg_ref[...]
    kn = k * lax.rsqrt(jnp.mean(k * k, axis=-1, keepdims=True) + EPS) * kg_ref[...]
    kmean = jnp.concatenate(
        [jnp.mean(kn[j * BS:(j + 1) * BS], axis=0, keepdims=True) for j in range(nb)], axis=0)
    knb = kn.astype(BF16)
    vb = v_ref[...].astype(BF16)
    t0 = t0_ref[...]
    t1 = t1_ref[...]
    far = far_ref[0:1, :]
    rows = lax.broadcasted_iota(jnp.int32, (BS, BS), 0)
    cols = lax.broadcasted_iota(jnp.int32, (BS, BS), 1)
    causal = cols <= rows
    blk_iota = lax.broadcasted_iota(jnp.int32, (BS, nb), 1)
    neg = -jnp.inf
    for qi in range(nb):
        qblk = qn[qi * BS:(qi + 1) * BS]
        sel = jnp.zeros((BS, nb), dtype=jnp.bool_)
        if qi > 0:
            gate = _dot_nt_hi(qblk, kmean)
            gate = jnp.where(blk_iota < qi, gate, neg)
            for _ in range(min(MOBA_TOPK, nb)):
                m = jnp.max(gate, axis=1, keepdims=True)
                idx = jnp.min(jnp.where(gate == m, blk_iota, nb), axis=1, keepdims=True)
                hit = (blk_iota == idx) & (m > neg)
                sel = sel | hit
                gate = jnp.where(blk_iota == idx, neg, gate)
        self32 = sel.astype(F32)
        s_all = _dot_nt(qblk, knb[:(qi + 1) * BS]) * scale
        blocks = []
        for j in range(qi + 1):
            s = s_all[:, j * BS:(j + 1) * BS]
            if j == qi:
                s = jnp.where(causal, s + t0, neg)
            else:
                bias = t1 if j == qi - 1 else far
                picked = self32[:, j:j + 1] > 0.5
                s = jnp.where(picked, s + bias, neg)
            blocks.append(s)
        m = jnp.max(blocks[-1], axis=1, keepdims=True)
        for s in blocks[:-1]:
            m = jnp.maximum(m, jnp.max(s, axis=1, keepdims=True))
        l = jnp.zeros((BS, 1), F32)
        acc = jnp.zeros((BS, A_HEAD_DIM), F32)
        for j, s in enumerate(blocks):
            p = jnp.exp(s - m)
            l = l + jnp.sum(p, axis=1, keepdims=True)
            acc = acc + _dot(p, vb[j * BS:(j + 1) * BS])
        o_ref[qi * BS:(qi + 1) * BS, :] = (acc / l).astype(o_ref.dtype)


def _moba(proj, q_gain, k_gain, t0, t1, far, B, S):
    H = A_HEADS
    nb = S // MOBA_BLOCK
    dh = A_HEAD_DIM
    return pl.pallas_call(
        functools.partial(_moba_kernel, nb=nb),
        grid=(B, H),
        in_specs=[
            pl.BlockSpec((S, dh), lambda b, h: (b, h)),
            pl.BlockSpec((S, dh), lambda b, h: (b, H + h)),
            pl.BlockSpec((S, dh), lambda b, h: (b, 2 * H + h)),
            pl.BlockSpec((1, dh), lambda b, h: (0, 0)),
            pl.BlockSpec((1, dh), lambda b, h: (0, 0)),
            pl.BlockSpec((None, MOBA_BLOCK, MOBA_BLOCK), lambda b, h: (h, 0, 0)),
            pl.BlockSpec((None, MOBA_BLOCK, MOBA_BLOCK), lambda b, h: (h, 0, 0)),
            pl.BlockSpec((None, 8, MOBA_BLOCK), lambda b, h: (h, 0, 0)),
        ],
        out_specs=pl.BlockSpec((S, dh), lambda b, h: (b, h)),
        out_shape=jax.ShapeDtypeStruct((B * S, H * dh), BF16),
        compiler_params=_cparams(("parallel", "parallel")),
        name="moba",
    )(proj, proj, proj, q_gain, k_gain, t0, t1, far)


def _t5_bucket(rel):
    n = jnp.maximum(-rel, 0)
    max_exact = REL_BUCKETS // 2
    nf = jnp.maximum(n, 1).astype(F32)
    large = max_exact + (jnp.log(nf / max_exact) / math.log(REL_MAX_DIST / max_exact)
                         * (REL_BUCKETS - max_exact)).astype(jnp.int32)
    large = jnp.minimum(large, REL_BUCKETS - 1)
    return jnp.where(n < max_exact, n, large)


def _bias_tiles(rel_bias):
    assert MOBA_BLOCK + 1 >= REL_MAX_DIST
    offs = jnp.arange(MOBA_BLOCK)
    rel0 = offs[None, :] - offs[:, None]
    bias_hb = rel_bias.T.astype(F32)

    def tile(rel):
        onehot = jax.nn.one_hot(_t5_bucket(rel), REL_BUCKETS, dtype=F32)
        return jnp.einsum('hb,qkb->hqk', bias_hb, onehot, precision=HIGHEST)

    t0 = tile(rel0)
    t1 = tile(rel0 - MOBA_BLOCK)
    far = jnp.broadcast_to(bias_hb[:, REL_BUCKETS - 1][:, None, None], (A_HEADS, 8, MOBA_BLOCK))
    return t0, t1, far


def _gdn_kernel(q_ref, k_ref, v_ref, z_ref, sm_ref, cwq_ref, cwk_ref, cwv_ref, alog_ref, dtb_ref,
                ng_ref, o_ref, qs, ks, vs, bs, gs, *, S):
    C = DELTA_CHUNK
    G = GDN_GROUP
    dk = B_KEY_DIM
    hh = pl.program_id(1)
    row = lax.broadcasted_iota(jnp.int32, (S, dk), 0)

    def conv_silu(x, w):
        y = x * w[CONV_WIDTH - 1:CONV_WIDTH, :]
        for d in range(1, CONV_WIDTH):
            xs = jnp.where(row >= d, pltpu.roll(x, d, 0), 0.0)
            y = y + xs * w[CONV_WIDTH - 1 - d:CONV_WIDTH - d, :]
        return _silu(y)

    def l2n(x):
        return x * lax.rsqrt(jnp.sum(x * x, axis=-1, keepdims=True) + EPS)

    qs[...] = l2n(conv_silu(q_ref[...], cwq_ref[...])) * (dk ** -0.5)
    ks[...] = l2n(conv_silu(k_ref[...], cwk_ref[...]))
    vs[...] = conv_silu(v_ref[...], cwv_ref[...])

    sm = sm_ref[...]
    beta_all = _sigmoid(sm)
    z = sm + dtb_ref[...]
    softplus = jnp.maximum(z, 0.0) + jnp.log(1.0 + jnp.exp(-jnp.abs(z)))
    g_all = -jnp.exp(alog_ref[...]) * softplus
    sel_r = lax.broadcasted_iota(jnp.int32, (LANES, LANES), 0)
    bs[...] = _dot_exact01_rhs(beta_all, sel_r == hh)
    gs[...] = _dot_exact01_rhs(g_all, sel_r == hh + B_HEADS)

    ri = lax.broadcasted_iota(jnp.int32, (G, G), 0)
    ci = lax.broadcasted_iota(jnp.int32, (G, G), 1)
    same = (ri // C) == (ci // C)
    strict = same & (ri > ci)
    causal = same & (ri >= ci)
    tri = causal
    strict_f = strict.astype(F32)
    eye = (ri == ci).astype(F32)
    ng = ng_ref[...]

    def group(gi, state):
        r0 = pl.multiple_of(gi * G, G)
        qg = qs[pl.ds(r0, G), :]
        kg = ks[pl.ds(r0, G), :]
        vg = vs[pl.ds(r0, G), :]
        bg = bs[pl.ds(r0, G), :]
        gg = gs[pl.ds(r0, G), :]
        gmat = jnp.concatenate([gg, gg], axis=1)
        dmat = _dot_exact01(tri, gmat * strict_f)
        gcum = _dot_exact01(tri, gg)
        decay = jnp.where(causal, jnp.exp(jnp.where(causal, dmat, 0.0)), 0.0)
        kb = kg * bg
        vb = vg * bg
        kk = _dot_nt(kb, kg)
        a = -(kk * decay) * strict_f
        t = eye + a
        p2 = _split2(a)
        for _ in range(int(math.log2(C)) - 1):
            p2 = _split2(_dot3(p2, p2))
            t = t + _dot3(_split2(t), p2)
        eg = jnp.exp(gcum)
        tb = t.astype(BF16)
        u = _mm(tb, vb.astype(BF16))
        w = _mm(tb, (kb * eg).astype(BF16))
        attn = _dot_nt(qg, kg) * decay
        qd = qg * eg
        outs = []
        vnews = []
        for c in range(G // C):
            sl = slice(c * C, (c + 1) * C)
            glast = gcum[(c + 1) * C - 1:(c + 1) * C, :]
            kd = kg[sl] * jnp.exp(glast - gcum[sl])
            sb = state.astype(BF16)
            v_new = u[sl] - _mm(w[sl].astype(BF16), sb)
            outs.append(_mm(qd[sl].astype(BF16), sb))
            vnews.append(v_new)
            state = state * jnp.exp(glast)[:, 0:1] + _dot_tn(kd, v_new)
        o = jnp.concatenate(outs, axis=0) + _dot(attn, jnp.concatenate(vnews, axis=0))
        on = o * lax.rsqrt(jnp.mean(o * o, axis=-1, keepdims=True) + EPS) * ng
        o_ref[pl.ds(r0, G), :] = (on * _silu(z_ref[pl.ds(r0, G), :])).astype(o_ref.dtype)
        return state

    lax.fori_loop(0, S // G, group, jnp.zeros((dk, B_VAL_DIM), F32))


def _gdn(proj, small, conv_w, alog_pad, dtb_pad, norm_gain, B, S):
    H = B_HEADS
    dk = B_KEY_DIM
    qoff = 3 * A_HEADS
    blk = lambda off: pl.BlockSpec((S, dk), lambda b, h: (b, off + h))
    cw = lambda off: pl.BlockSpec((CONV_WIDTH, dk), lambda b, h: (0, off + h))
    one = pl.BlockSpec((1, LANES), lambda b, h: (0, 0))
    return pl.pallas_call(
        functools.partial(_gdn_kernel, S=S),
        grid=(B, H),
        in_specs=[blk(qoff), blk(qoff + H), blk(qoff + 2 * H), blk(qoff + 3 * H),
                  pl.BlockSpec((S, LANES), lambda b, h: (b, 0)),
                  cw(0), cw(H), cw(2 * H), one, one, one],
        out_specs=pl.BlockSpec((S, dk), lambda b, h: (b, h)),
        out_shape=jax.ShapeDtypeStruct((B * S, H * B_VAL_DIM), BF16),
        scratch_shapes=[pltpu.VMEM((S, dk), F32)] * 5,
        compiler_params=_cparams(("parallel", "parallel")),
        name="gdn",
    )(proj, proj, proj, proj, small, conv_w, conv_w, conv_w, alog_pad, dtb_pad, norm_gain)


def _merge_kernel(oa_ref, ob_ref, wa_ref, wb_ref, ga_ref, gb_ref, o_ref):
    a = jnp.dot(oa_ref[...], wa_ref[...], preferred_element_type=F32)
    b = jnp.dot(ob_ref[...], wb_ref[...], preferred_element_type=F32)
    o_ref[...] = (_sigmoid(ga_ref[...]) * a + _sigmoid(gb_ref[...]) * b).astype(o_ref.dtype)


def _merge(o_a, o_b, w_up_a, w_up_b, proj, tm=512, tn=1024):
    T, K = o_a.shape
    D = w_up_a.shape[1]
    ga_off = (3 * A_HEADS + 4 * B_HEADS) * LANES // tn
    gb_off = ga_off + D // tn
    return pl.pallas_call(
        _merge_kernel,
        grid=(T // tm, D // tn),
        in_specs=[
            pl.BlockSpec((tm, K), lambda i, j: (i, 0)),
            pl.BlockSpec((tm, K), lambda i, j: (i, 0)),
            pl.BlockSpec((K, tn), lambda i, j: (0, j)),
            pl.BlockSpec((K, tn), lambda i, j: (0, j)),
            pl.BlockSpec((tm, tn), lambda i, j: (i, ga_off + j)),
            pl.BlockSpec((tm, tn), lambda i, j: (i, gb_off + j)),
        ],
        out_specs=pl.BlockSpec((tm, tn), lambda i, j: (i, j)),
        out_shape=jax.ShapeDtypeStruct((T, D), BF16),
        compiler_params=_cparams(("parallel", "parallel")),
        name="merge",
    )(o_a, o_b, w_up_a, w_up_b, proj, proj)


def _out_proj_kernel(m_ref, w_ref, x_ref, g_ref, y_ref, h_ref):
    y = x_ref[...] + jnp.dot(m_ref[...], w_ref[...], preferred_element_type=F32)
    y_ref[...] = y
    ms = jnp.mean(y * y, axis=-1, keepdims=True)
    h_ref[...] = (y * lax.rsqrt(ms + EPS) * g_ref[...]).astype(h_ref.dtype)


def _out_proj(merged, w_out, x2, gain2, tm=512):
    T, D = x2.shape
    return pl.pallas_call(
        _out_proj_kernel,
        grid=(T // tm,),
        in_specs=[
            pl.BlockSpec((tm, D), lambda i: (i, 0)),
            pl.BlockSpec((D, D), lambda i: (0, 0)),
            pl.BlockSpec((tm, D), lambda i: (i, 0)),
            pl.BlockSpec((1, D), lambda i: (0, 0)),
        ],
        out_specs=[pl.BlockSpec((tm, D), lambda i: (i, 0)), pl.BlockSpec((tm, D), lambda i: (i, 0))],
        out_shape=[jax.ShapeDtypeStruct((T, D), F32), jax.ShapeDtypeStruct((T, D), BF16)],
        compiler_params=_cparams(("parallel",)),
        name="out_proj",
    )(merged, w_out, x2, gain2)


def _matmul_kernel(a_ref, b_ref, o_ref):
    o_ref[...] = jnp.dot(a_ref[...], b_ref[...], preferred_element_type=F32)


def _peer_query(h2, w_query, tm=512):
    T, D = h2.shape
    N = w_query.shape[1]
    return pl.pallas_call(
        _matmul_kernel,
        grid=(T // tm,),
        in_specs=[pl.BlockSpec((tm, D), lambda i: (i, 0)), pl.BlockSpec((D, N), lambda i: (0, 0))],
        out_specs=pl.BlockSpec((tm, N), lambda i: (i, 0)),
        out_shape=jax.ShapeDtypeStruct((T, N), F32),
        compiler_params=_cparams(("parallel",)),
        name="peer_query",
    )(h2, w_query)


_NO_POS = 2 ** 30


def _topk_rows(s, k, pos):
    vals, idxs = [], []
    for _ in range(k):
        m = jnp.max(s, axis=0, keepdims=True)
        i = jnp.min(jnp.where(s == m, pos, _NO_POS), axis=0, keepdims=True)
        vals.append(m)
        idxs.append(i)
        s = jnp.where(pos == i, -jnp.inf, s)
    return jnp.concatenate(vals, axis=0), jnp.concatenate(idxs, axis=0)


def _peer_route_kernel(q_ref, keys_ref, idx_ref, gate_ref, *, tt):
    K = PEER_TOPK
    half = PEER_QDIM // 2
    pos_k = lax.broadcasted_iota(jnp.int32, (PEER_NKEYS, tt), 0)
    row_k = lax.broadcasted_iota(jnp.int32, (K, tt), 0)
    row_8 = lax.broadcasted_iota(jnp.int32, (8, tt), 0)
    assert K == 16 and all((i < 4) or (i < 8 and j < 4) or (j == 0)
                           for i in range(K) for j in range(K) if (i + 1) * (j + 1) <= K)
    idx_rows, gate_rows = [], []
    for h in range(PEER_HEADS):
        tops = []
        for p in range(2):
            qh = q_ref[:, (2 * h + p) * half:(2 * h + p + 1) * half]
            sc = _dot_nt_hi(keys_ref[h, p], qh)
            tops.append(_topk_rows(sc, K, pos_k))
        (s0, i0), (s1, i1) = tops
        cb, pb, ib = [], [], []
        for i in range(4):
            cb.append(s0[i:i + 1] + s1)
            pb.append(i * K + row_k)
            ib.append(i0[i:i + 1] * PEER_NKEYS + i1)
        for j in range(4):
            cb.append(jnp.where(row_8 >= 4, s0[0:8] + s1[j:j + 1], -jnp.inf))
            pb.append(row_8 * K + j)
            ib.append(i0[0:8] * PEER_NKEYS + i1[j:j + 1])
        cb.append(s0[8:16] + s1[0:1])
        pb.append((row_8 + 8) * K)
        ib.append(i0[8:16] * PEER_NKEYS + i1[0:1])
        cand = jnp.concatenate(cb, axis=0)
        pos_c = jnp.concatenate(pb, axis=0)
        cidx = jnp.concatenate(ib, axis=0)
        best, bpos = _topk_rows(cand, K, pos_c)
        eidx = jnp.concatenate(
            [jnp.min(jnp.where(pos_c == bpos[r:r + 1], cidx, PEER_NKEYS * PEER_NKEYS),
                     axis=0, keepdims=True) for r in range(K)], axis=0)
        e = jnp.exp(best - best[0:1])
        g = e / jnp.sum(e, axis=0, keepdims=True)
        zero = jnp.zeros_like(g[0:1])
        for r in range(K):
            gate_rows += [zero, g[r:r + 1]]
        idx_rows.append(eidx)
    idx_ref[...] = jnp.concatenate(idx_rows, axis=0).T
    gate_ref[...] = jnp.concatenate(gate_rows, axis=0).T


def _peer_route(qry, sub_keys, tt=256):
    T, N = qry.shape
    PK = PEER_HEADS * PEER_TOPK
    return pl.pallas_call(
        functools.partial(_peer_route_kernel, tt=tt),
        grid=(T // tt,),
        in_specs=[pl.BlockSpec((tt, N), lambda i: (i, 0)),
                  pl.BlockSpec(sub_keys.shape, lambda i: (0, 0, 0, 0))],
        out_specs=[pl.BlockSpec((tt, PK), lambda i: (i, 0)), pl.BlockSpec((tt, 2 * PK), lambda i: (i, 0))],
        out_shape=[jax.ShapeDtypeStruct((T, PK), jnp.int32), jax.ShapeDtypeStruct((T, 2 * PK), F32)],
        compiler_params=_cparams(("parallel",)),
        name="peer_route",
    )(qry, sub_keys)


PEER_TOKENS_PER_STEP = 4
PEER_AHEAD = 2
PEER_TOKEN_GROUP = 16
PEER_BANKS = PEER_TOKEN_GROUP // PEER_TOKENS_PER_STEP
PEER_SLOTS = PEER_BANKS * PEER_TOKENS_PER_STEP


def _gelu(x):
    return 0.5 * x * (1.0 + lax.erf(x * (2.0 ** -0.5)))


def _pack_expert_table(u, v):
    def pack(x):
        E, D = x.shape
        b = lax.bitcast_convert_type(x.astype(BF16), jnp.uint16).astype(jnp.uint32)
        return (b[:, :D // 2] | (b[:, D // 2:] << 16)).reshape(E, D // (2 * LANES), LANES)
    return jnp.concatenate([pack(u), pack(v)], axis=1)


def _peer_expert_kernel(idx_ref, gate_ref, h_ref, y_ref, tab_ref, o_ref, buf, sem, *, tt, D):
    PK = PEER_HEADS * PEER_TOPK
    GT = PEER_TOKEN_GROUP
    HD = D // 2
    R = HD // LANES

    def issue(t, slot, k0=0, k1=PK):
        for k in range(k0, k1):
            e = idx_ref[t, k]
            pltpu.make_async_copy(tab_ref.at[e], buf.at[slot, :, k], sem.at[slot]).start()

    def wait(slot):
        pltpu.make_async_copy(buf.at[slot], buf.at[slot], sem.at[slot]).wait()

    TPS = PEER_TOKENS_PER_STEP

    AHEAD = PEER_AHEAD
    NSTEP = GT // TPS
    assert AHEAD < PEER_BANKS

    for t in range(AHEAD * TPS):
        issue(t, t)

    even = lax.broadcasted_iota(jnp.int32, (1, 2 * PK), 1) % 2 == 0

    def token(j, slot, hr, gg, prefetch):
        z = jnp.zeros((1, 2 * PK), F32)
        for r in range(R):
            ub = pltpu.bitcast(buf[slot, r], BF16)
            res = lax.dot_general(hr[r], ub, (((1,), (1,)), ((), ())), preferred_element_type=F32)
            z = z + jnp.where(even, res[j:j + 1], res[GT + j:GT + j + 1])
            prefetch(r)
        s = z + pltpu.roll(z, 1, 1)
        a = _gelu(s) * gg[j:j + 1]
        a2 = jnp.concatenate([pltpu.roll(a, 2 * PK - 1, 1), a], axis=0).astype(BF16)
        o2 = []
        for r in range(R):
            o2.append(jnp.dot(a2, pltpu.bitcast(buf[slot, R + r], BF16), preferred_element_type=F32))
            prefetch(R + r)
        return jnp.concatenate([o[0:1] for o in o2] + [o[1:2] for o in o2], axis=1)

    def group(t0, last):
        hg = h_ref[pl.ds(t0, GT), :]
        gg = gate_ref[pl.ds(t0, GT), :]
        hr = [jnp.concatenate([hg[:, r * LANES:(r + 1) * LANES],
                               hg[:, HD + r * LANES:HD + (r + 1) * LANES]], axis=0) for r in range(R)]
        outs = []
        for step in range(NSTEP):
            bank, nbank = step * TPS, ((step + AHEAD) % NSTEP) * TPS
            for i in range(TPS):
                wait(bank + i)
            for i in range(TPS):
                def prefetch(stage, i=i, step=step, nbank=nbank):
                    if not (last and step + AHEAD >= NSTEP):
                        per = PK // (2 * R)
                        issue(t0 + (step + AHEAD) * TPS + i, nbank + i, stage * per, (stage + 1) * per)
                outs.append(token(step * TPS + i, bank + i, hr, gg, prefetch))
        o_ref[pl.ds(t0, GT), :] = y_ref[pl.ds(t0, GT), :] + jnp.concatenate(outs, axis=0)

    def body(gi, c):
        group(pl.multiple_of(gi * GT, GT), False)
        return c

    lax.fori_loop(0, tt // GT - 1, body, 0)
    group(tt - GT, True)


def _peer_experts(idx, gate2, h2, y1, table, tt=256):
    T, D = y1.shape
    PK = PEER_HEADS * PEER_TOPK
    return pl.pallas_call(
        functools.partial(_peer_expert_kernel, tt=tt, D=D),
        grid=(T // tt,),
        in_specs=[
            pl.BlockSpec((tt, PK), lambda i: (i, 0), memory_space=pltpu.SMEM),
            pl.BlockSpec((tt, 2 * PK), lambda i: (i, 0)),
            pl.BlockSpec((tt, D), lambda i: (i, 0)),
            pl.BlockSpec((tt, D), lambda i: (i, 0)),
            pl.BlockSpec(memory_space=pl.ANY),
        ],
        out_specs=pl.BlockSpec((tt, D), lambda i: (i, 0)),
        out_shape=jax.ShapeDtypeStruct((T, D), F32),
        scratch_shapes=[pltpu.VMEM((PEER_SLOTS, table.shape[1], PK, LANES), jnp.uint32),
                        pltpu.SemaphoreType.DMA((PEER_SLOTS,))],
        compiler_params=pltpu.CompilerParams(dimension_semantics=("arbitrary",),
                                             vmem_limit_bytes=VMEM_LIMIT, disable_bounds_checks=True),
        name="peer_experts",
    )(idx, gate2, h2, y1, table)


def _layer(x, rel_bias, norm1_gain, w_in, conv_w, a_log, dt_bias, q_norm_gain, k_norm_gain,
           gdn_norm_gain, w_up_a, w_up_b, w_out, norm2_gain, peer_w_query, peer_sub_keys,
           peer_u, peer_v):
    B, S, D = x.shape
    T = B * S
    x2 = x.reshape(T, D)
    n_main = 3 * A_HEADS * A_HEAD_DIM + 4 * B_HEADS * B_KEY_DIM
    n_small = 2 * B_HEADS
    w_main = jnp.concatenate([w_in[:, :n_main], w_in[:, n_main + n_small:]], axis=1).astype(BF16)
    w_small = jnp.pad(w_in[:, n_main:n_main + n_small], ((0, 0), (0, LANES - n_small))).astype(BF16)
    proj, small = _in_proj(x2, norm1_gain[None, :], w_main, w_small)

    t0, t1, far = _bias_tiles(rel_bias)
    o_a = _moba(proj, q_norm_gain[None, :], k_norm_gain[None, :], t0, t1, far, B, S)

    pad8 = lambda v: jnp.pad(v, (B_HEADS, LANES - 2 * B_HEADS))[None, :]
    o_b = _gdn(proj, small, conv_w, pad8(a_log), pad8(dt_bias), gdn_norm_gain[None, :], B, S)

    merged = _merge(o_a, o_b, w_up_a.astype(BF16), w_up_b.astype(BF16), proj)
    y1, h2 = _out_proj(merged, w_out.astype(BF16), x2, norm2_gain[None, :])

    qry = _peer_query(h2, peer_w_query.astype(BF16))
    idx, gate2 = _peer_route(qry, peer_sub_keys)
    y = _peer_experts(idx, gate2, h2, y1, _pack_expert_table(peer_u, peer_v))
    return y.reshape(B, S, D)


def kernel(x, rel_bias, norm1_gain, w_in, conv_w, a_log, dt_bias, q_norm_gain, k_norm_gain,
           gdn_norm_gain, w_up_a, w_up_b, w_out, norm2_gain, peer_w_query, peer_sub_keys,
           peer_u, peer_v):
    depth = w_in.shape[0]
    for l in range(depth):
        x = _layer(x, rel_bias, norm1_gain[l], w_in[l], conv_w[l], a_log[l], dt_bias[l],
                   q_norm_gain[l], k_norm_gain[l], gdn_norm_gain[l], w_up_a[l], w_up_b[l],
                   w_out[l], norm2_gain[l], peer_w_query[l], peer_sub_keys[l], peer_u[l], peer_v[l])
    return x
```

```python
import functools
import math

import jax
import jax.numpy as jnp
import numpy as np
from jax import lax
from jax.experimental import pallas as pl
from jax.experimental.pallas import tpu as pltpu

F32 = jnp.float32
BF16 = jnp.bfloat16
HIGHEST = lax.Precision.HIGHEST

LANES = 128
EPS = 1e-6

A_HEADS = 8
A_HEAD_DIM = 128
MOBA_BLOCK = 256
MOBA_TOPK = 3
REL_BUCKETS = 32
REL_MAX_DIST = 128
B_HEADS = 8
B_KEY_DIM = 128
B_VAL_DIM = 128
CONV_WIDTH = 4
DELTA_CHUNK = 64
GDN_GROUP = 256
GDN_HEADS_PER_STEP = 2
PEER_HEADS = 8
PEER_NKEYS = 128
PEER_QDIM = 256
PEER_TOPK = 16

VMEM_LIMIT = 56 * 1024 * 1024


def _cparams(sem):
    return pltpu.CompilerParams(dimension_semantics=sem, vmem_limit_bytes=VMEM_LIMIT)


def _dot(a, b):
    return jnp.dot(a.astype(BF16), b.astype(BF16), preferred_element_type=F32)


def _dot_nt(a, b):
    return lax.dot_general(a.astype(BF16), b.astype(BF16), (((1,), (1,)), ((), ())),
                           preferred_element_type=F32)


def _dot_tn(a, b):
    return lax.dot_general(a.astype(BF16), b.astype(BF16), (((0,), (0,)), ((), ())),
                           preferred_element_type=F32)


def _dot_hi(a, b):
    return jnp.dot(a, b, preferred_element_type=F32, precision=HIGHEST)


def _dot_nt_hi(a, b):
    return lax.dot_general(a, b, (((1,), (1,)), ((), ())), preferred_element_type=F32,
                           precision=HIGHEST)


def _split2(a):
    hi = a.astype(BF16)
    return hi, (a - hi.astype(F32)).astype(BF16)


def _mm(a, b):
    return jnp.dot(a, b, preferred_element_type=F32)


def _dot3(a2, b2):
    (ah, al), (bh, bl) = a2, b2
    return _mm(ah, bh) + (_mm(ah, bl) + _mm(al, bh))


def _dot_exact01(m01, x):
    m = m01.astype(BF16)
    hi = x.astype(BF16)
    r1 = x - hi.astype(F32)
    mid = r1.astype(BF16)
    lo = (r1 - mid.astype(F32)).astype(BF16)
    return _mm(m, hi) + (_mm(m, mid) + _mm(m, lo))


def _dot_exact01_rhs(x, m01):
    m = m01.astype(BF16)
    hi = x.astype(BF16)
    r1 = x - hi.astype(F32)
    mid = r1.astype(BF16)
    lo = (r1 - mid.astype(F32)).astype(BF16)
    return _mm(hi, m) + (_mm(mid, m) + _mm(lo, m))


def _sigmoid(x):
    return 1.0 / (1.0 + jnp.exp(-x))


def _silu(x):
    return x * _sigmoid(x)


def _in_proj_kernel(x_ref, g_ref, w_ref, ws_ref, o_ref, os_ref, h_scr):
    @pl.when(pl.program_id(1) == 0)
    def _():
        x = x_ref[...]
        ms = jnp.mean(x * x, axis=-1, keepdims=True)
        h = (x * lax.rsqrt(ms + EPS) * g_ref[...]).astype(BF16)
        h_scr[...] = h
        os_ref[...] = jnp.dot(h, ws_ref[...], preferred_element_type=F32)

    o_ref[...] = jnp.dot(h_scr[...], w_ref[...], preferred_element_type=F32)


def _in_proj(x2, gain, w_main, w_small, tm=1024, tn=1024):
    T, D = x2.shape
    N = w_main.shape[1]
    return pl.pallas_call(
        _in_proj_kernel,
        grid=(T // tm, N // tn),
        in_specs=[
            pl.BlockSpec((tm, D), lambda i, j: (i, 0)),
            pl.BlockSpec((1, D), lambda i, j: (0, 0)),
            pl.BlockSpec((D, tn), lambda i, j: (0, j)),
            pl.BlockSpec((D, LANES), lambda i, j: (0, 0)),
        ],
        out_specs=[
            pl.BlockSpec((tm, tn), lambda i, j: (i, j)),
            pl.BlockSpec((tm, LANES), lambda i, j: (i, 0)),
        ],
        out_shape=[jax.ShapeDtypeStruct((T, N), F32), jax.ShapeDtypeStruct((T, LANES), F32)],
        scratch_shapes=[pltpu.VMEM((tm, D), BF16)],
        compiler_params=_cparams(("parallel", "arbitrary")),
        name="in_proj",
    )(x2, gain, w_main, w_small)


def _moba_kernel(q_ref, k_ref, v_ref, qg_ref, kg_ref, t0_ref, t1_ref, far_ref, o_ref, *, nb):
    BS = MOBA_BLOCK
    scale = A_HEAD_DIM ** -0.5
    q = q_ref[...]
    k = k_ref[...]
    qn = q * lax.rsqrt(jnp.mean(q * q, axis=-1, keepdims=True) + EPS) * qg_ref[...]
    kn = k * lax.rsqrt(jnp.mean(k * k, axis=-1, keepdims=True) + EPS) * kg_ref[...]
    kmean = jnp.concatenate(
        [jnp.mean(kn[j * BS:(j + 1) * BS], axis=0, keepdims=True) for j in range(nb)], axis=0)
    knb = kn.astype(BF16)
    vb = v_ref[...].astype(BF16)
    t0 = t0_ref[...]
    t1 = t1_ref[...]
    far = far_ref[0:1, :]
    rows = lax.broadcasted_iota(jnp.int32, (BS, BS), 0)
    cols = lax.broadcasted_iota(jnp.int32, (BS, BS), 1)
    causal = cols <= rows
    blk_iota = lax.broadcasted_iota(jnp.int32, (BS, nb), 1)
    neg = -jnp.inf
    for qi in range(nb):
        qblk = qn[qi * BS:(qi + 1) * BS]
        sel = jnp.zeros((BS, nb), dtype=jnp.bool_)
        if qi > 0:
            gate = _dot_nt_hi(qblk, kmean)
            gate = jnp.where(blk_iota < qi, gate, neg)
            for _ in range(min(MOBA_TOPK, nb)):
                m = jnp.max(gate, axis=1, keepdims=True)
                idx = jnp.min(jnp.where(gate == m, blk_iota, nb), axis=1, keepdims=True)
                hit = (blk_iota == idx) & (m > neg)
                sel = sel | hit
                gate = jnp.where(blk_iota == idx, neg, gate)
        self32 = sel.astype(F32)
        s_all = _dot_nt(qblk, knb[:(qi + 1) * BS]) * scale
        blocks = []
        for j in range(qi + 1):
            s = s_all[:, j * BS:(j + 1) * BS]
            if j == qi:
                s = jnp.where(causal, s + t0, neg)
            else:
                bias = t1 if j == qi - 1 else far
                picked = self32[:, j:j + 1] > 0.5
                s = jnp.where(picked, s + bias, neg)
            blocks.append(s)
        smax = blocks[-1]
        for s in blocks[:-1]:
            smax = jnp.maximum(smax, s)
        m = jnp.max(smax, axis=1, keepdims=True)
        psum = jnp.zeros((BS, BS), F32)
        acc = jnp.zeros((BS, A_HEAD_DIM), F32)
        for j, s in enumerate(blocks):
            p = jnp.exp(s - m)
            psum = psum + p
            acc = acc + _dot(p, vb[j * BS:(j + 1) * BS])
        l = jnp.sum(psum, axis=1, keepdims=True)
        o_ref[qi * BS:(qi + 1) * BS, :] = (acc / l).astype(o_ref.dtype)


def _moba(proj, q_gain, k_gain, t0, t1, far, B, S):
    H = A_HEADS
    nb = S // MOBA_BLOCK
    dh = A_HEAD_DIM
    return pl.pallas_call(
        functools.partial(_moba_kernel, nb=nb),
        grid=(B, H),
        in_specs=[
            pl.BlockSpec((S, dh), lambda b, h: (b, h)),
            pl.BlockSpec((S, dh), lambda b, h: (b, H + h)),
            pl.BlockSpec((S, dh), lambda b, h: (b, 2 * H + h)),
            pl.BlockSpec((1, dh), lambda b, h: (0, 0)),
            pl.BlockSpec((1, dh), lambda b, h: (0, 0)),
            pl.BlockSpec((None, MOBA_BLOCK, MOBA_BLOCK), lambda b, h: (h, 0, 0)),
            pl.BlockSpec((None, MOBA_BLOCK, MOBA_BLOCK), lambda b, h: (h, 0, 0)),
            pl.BlockSpec((None, 8, MOBA_BLOCK), lambda b, h: (h, 0, 0)),
        ],
        out_specs=pl.BlockSpec((S, dh), lambda b, h: (b, h)),
        out_shape=jax.ShapeDtypeStruct((B * S, H * dh), BF16),
        compiler_params=_cparams(("parallel", "parallel")),
        name="moba",
    )(proj, proj, proj, q_gain, k_gain, t0, t1, far)


def _t5_bucket(rel):
    n = jnp.maximum(-rel, 0)
    max_exact = REL_BUCKETS // 2
    nf = jnp.maximum(n, 1).astype(F32)
    large = max_exact + (jnp.log(nf / max_exact) / math.log(REL_MAX_DIST / max_exact)
                         * (REL_BUCKETS - max_exact)).astype(jnp.int32)
    large = jnp.minimum(large, REL_BUCKETS - 1)
    return jnp.where(n < max_exact, n, large)


def _bias_tiles(rel_bias):
    assert MOBA_BLOCK + 1 >= REL_MAX_DIST
    offs = jnp.arange(MOBA_BLOCK)
    rel0 = offs[None, :] - offs[:, None]
    bias_hb = rel_bias.T.astype(F32)

    def tile(rel):
        onehot = jax.nn.one_hot(_t5_bucket(rel), REL_BUCKETS, dtype=F32)
        return jnp.einsum('hb,qkb->hqk', bias_hb, onehot, precision=HIGHEST)

    t0 = tile(rel0)
    t1 = tile(rel0 - MOBA_BLOCK)
    far = jnp.broadcast_to(bias_hb[:, REL_BUCKETS - 1][:, None, None], (A_HEADS, 8, MOBA_BLOCK))
    return t0, t1, far


def _gdn_kernel(q_ref, k_ref, v_ref, z_ref, sm_ref, cwq_ref, cwk_ref, cwv_ref, alog_ref, dtb_ref,
                ng_ref, o_ref, qs, ks, vs, bs, gs, *, S):
    C = DELTA_CHUNK
    G = GDN_GROUP
    dk = B_KEY_DIM
    HP = GDN_HEADS_PER_STEP
    NG = S // G
    sel_r = lax.broadcasted_iota(jnp.int32, (LANES, LANES), 0)

    def conv_silu(x_ref, w_ref, r0, first):
        w = w_ref[...]
        cur = x_ref[pl.ds(r0, G), :]
        above = x_ref[pl.ds(pl.multiple_of(jnp.maximum(r0 - 8, 0), 8), 8), :]
        xcat = jnp.concatenate([jnp.where(first, 0.0, above), cur], axis=0)
        y = cur * w[CONV_WIDTH - 1:CONV_WIDTH, :]
        for d in range(1, CONV_WIDTH):
            y = y + pltpu.roll(xcat, d, 0)[8:] * w[CONV_WIDTH - 1 - d:CONV_WIDTH - d, :]
        return _silu(y)

    def l2n(x):
        parts = [x[:, i * dk:(i + 1) * dk] for i in range(HP)]
        return jnp.concatenate(
            [p * lax.rsqrt(jnp.sum(p * p, axis=-1, keepdims=True) + EPS) for p in parts], axis=1)

    def prepare(gi, slot):
        r0 = pl.multiple_of(gi * G, G)
        first = gi == 0
        qs[slot] = l2n(conv_silu(q_ref, cwq_ref, r0, first)) * (dk ** -0.5)
        yield
        ks[slot] = l2n(conv_silu(k_ref, cwk_ref, r0, first))
        yield
        vs[slot] = conv_silu(v_ref, cwv_ref, r0, first)
        yield
        sm = sm_ref[pl.ds(r0, G), :]
        beta_all = _sigmoid(sm)
        z = sm + dtb_ref[...]
        softplus = jnp.maximum(z, 0.0) + jnp.log(1.0 + jnp.exp(-jnp.abs(z)))
        g_all = -jnp.exp(alog_ref[...]) * softplus
        for i in range(HP):
            hh = pl.program_id(1) * HP + i
            bs[slot, i] = _dot_exact01_rhs(beta_all, sel_r == hh)
            gs[slot, i] = _dot_exact01_rhs(g_all, sel_r == hh + B_HEADS)
            yield

    for _ in prepare(jnp.int32(0), 0):
        pass

    ri = lax.broadcasted_iota(jnp.int32, (G, G), 0)
    ci = lax.broadcasted_iota(jnp.int32, (G, G), 1)
    same = (ri // C) == (ci // C)
    strict = same & (ri > ci)
    causal = same & (ri >= ci)
    tri = causal
    strict_f = strict.astype(F32)
    eye = (ri == ci).astype(F32)
    ng = ng_ref[...]

    def group(gi, states):
        r0 = pl.multiple_of(gi * G, G)
        slot = gi % 2
        gens = [head_group(r0, slot, i, states[i]) for i in range(HP)]
        gens.append(prepare(jnp.minimum(gi + 1, NG - 1), 1 - slot))
        done = {}
        while len(done) < len(gens):
            for i in range(len(gens)):
                if i not in done:
                    try:
                        next(gens[i])
                    except StopIteration as stop:
                        done[i] = stop.value
        return tuple(done[i] for i in range(HP))

    def head_group(r0, slot, i, state):
        lanes = slice(i * dk, (i + 1) * dk)
        qg = qs[slot, :, lanes]
        kg = ks[slot, :, lanes]
        vg = vs[slot, :, lanes]
        bg = bs[slot, i]
        gg = gs[slot, i]
        gmat = jnp.concatenate([gg, gg], axis=1)
        dmat = _dot_exact01(tri, gmat * strict_f)
        gcum = _dot_exact01(tri, gg)
        yield
        decay = jnp.where(causal, jnp.exp(jnp.where(causal, dmat, 0.0)), 0.0)
        kb = kg * bg
        vb = vg * bg
        kk = _dot_nt(kb, kg)
        yield
        a = -(kk * decay) * strict_f
        t = eye + a
        p2 = _split2(a)
        for _ in range(int(math.log2(C)) - 1):
            p2 = _split2(_dot3(p2, p2))
            yield
            t = t + _dot3(_split2(t), p2)
            yield
        eg = jnp.exp(gcum)
        tb = t.astype(BF16)
        u = _mm(tb, vb.astype(BF16))
        w = _mm(tb, (kb * eg).astype(BF16))
        attn = _dot_nt(qg, kg) * decay
        yield
        qd = qg * eg
        outs = []
        vnews = []
        for c in range(G // C):
            sl = slice(c * C, (c + 1) * C)
            glast = gcum[(c + 1) * C - 1:(c + 1) * C, :]
            kd = kg[sl] * jnp.exp(glast - gcum[sl])
            sb = state.astype(BF16)
            v_new = u[sl] - _mm(w[sl].astype(BF16), sb)
            outs.append(_mm(qd[sl].astype(BF16), sb))
            vnews.append(v_new)
            yield
            state = state * jnp.exp(glast)[:, 0:1] + _dot_tn(kd, v_new)
            yield
        o = jnp.concatenate(outs, axis=0) + _dot(attn, jnp.concatenate(vnews, axis=0))
        on = o * lax.rsqrt(jnp.mean(o * o, axis=-1, keepdims=True) + EPS) * ng
        o_ref[pl.ds(r0, G), lanes] = (on * _silu(z_ref[pl.ds(r0, G), lanes])).astype(o_ref.dtype)
        return state

    lax.fori_loop(0, S // G, group, tuple(jnp.zeros((dk, B_VAL_DIM), F32) for _ in range(HP)))


def _gdn(proj, small, conv_w, alog_pad, dtb_pad, norm_gain, B, S):
    HP = GDN_HEADS_PER_STEP
    NH = B_HEADS // HP
    W = HP * B_KEY_DIM
    qoff = 3 * A_HEADS // HP
    blk = lambda off: pl.BlockSpec((S, W), lambda b, h: (b, off + h))
    cw = lambda off: pl.BlockSpec((CONV_WIDTH, W), lambda b, h: (0, off + h))
    one = pl.BlockSpec((1, LANES), lambda b, h: (0, 0))
    return pl.pallas_call(
        functools.partial(_gdn_kernel, S=S),
        grid=(B, NH),
        in_specs=[blk(qoff), blk(qoff + NH), blk(qoff + 2 * NH), blk(qoff + 3 * NH),
                  pl.BlockSpec((S, LANES), lambda b, h: (b, 0)),
                  cw(0), cw(NH), cw(2 * NH),
                  one, one, one],
        out_specs=pl.BlockSpec((S, W), lambda b, h: (b, h)),
        out_shape=jax.ShapeDtypeStruct((B * S, B_HEADS * B_VAL_DIM), BF16),
        scratch_shapes=([pltpu.VMEM((2, GDN_GROUP, W), F32)] * 3
                        + [pltpu.VMEM((2, HP, GDN_GROUP, LANES), F32)] * 2),
        compiler_params=_cparams(("parallel", "parallel")),
        name="gdn",
    )(proj, proj, proj, proj, small, conv_w, conv_w, conv_w, alog_pad, dtb_pad, norm_gain)


def _merge_kernel(oa_ref, ob_ref, wa_ref, wb_ref, ga_ref, gb_ref, o_ref):
    a = jnp.dot(oa_ref[...], wa_ref[...], preferred_element_type=F32)
    b = jnp.dot(ob_ref[...], wb_ref[...], preferred_element_type=F32)
    o_ref[...] = (_sigmoid(ga_ref[...]) * a + _sigmoid(gb_ref[...]) * b).astype(o_ref.dtype)


def _merge(o_a, o_b, w_up_a, w_up_b, proj, tm=512, tn=1024):
    T, K = o_a.shape
    D = w_up_a.shape[1]
    ga_off = (3 * A_HEADS + 4 * B_HEADS) * LANES // tn
    gb_off = ga_off + D // tn
    return pl.pallas_call(
        _merge_kernel,
        grid=(T // tm, D // tn),
        in_specs=[
            pl.BlockSpec((tm, K), lambda i, j: (i, 0)),
            pl.BlockSpec((tm, K), lambda i, j: (i, 0)),
            pl.BlockSpec((K, tn), lambda i, j: (0, j)),
            pl.BlockSpec((K, tn), lambda i, j: (0, j)),
            pl.BlockSpec((tm, tn), lambda i, j: (i, ga_off + j)),
            pl.BlockSpec((tm, tn), lambda i, j: (i, gb_off + j)),
        ],
        out_specs=pl.BlockSpec((tm, tn), lambda i, j: (i, j)),
        out_shape=jax.ShapeDtypeStruct((T, D), BF16),
        compiler_params=_cparams(("parallel", "parallel")),
        name="merge",
    )(o_a, o_b, w_up_a, w_up_b, proj, proj)


def _out_proj_kernel(m_ref, w_ref, x_ref, g_ref, y_ref, h_ref):
    y = x_ref[...] + jnp.dot(m_ref[...], w_ref[...], preferred_element_type=F32)
    y_ref[...] = y
    ms = jnp.mean(y * y, axis=-1, keepdims=True)
    h_ref[...] = (y * lax.rsqrt(ms + EPS) * g_ref[...]).astype(h_ref.dtype)


def _out_proj(merged, w_out, x2, gain2, tm=512):
    T, D = x2.shape
    return pl.pallas_call(
        _out_proj_kernel,
        grid=(T // tm,),
        in_specs=[
            pl.BlockSpec((tm, D), lambda i: (i, 0)),
            pl.BlockSpec((D, D), lambda i: (0, 0)),
            pl.BlockSpec((tm, D), lambda i: (i, 0)),
            pl.BlockSpec((1, D), lambda i: (0, 0)),
        ],
        out_specs=[pl.BlockSpec((tm, D), lambda i: (i, 0)), pl.BlockSpec((tm, D), lambda i: (i, 0))],
        out_shape=[jax.ShapeDtypeStruct((T, D), F32), jax.ShapeDtypeStruct((T, D), BF16)],
        compiler_params=_cparams(("parallel",)),
        name="out_proj",
    )(merged, w_out, x2, gain2)


def _matmul_kernel(a_ref, b_ref, o_ref):
    o_ref[...] = jnp.dot(a_ref[...], b_ref[...], preferred_element_type=F32)


def _peer_query(h2, w_query, tm=512):
    T, D = h2.shape
    N = w_query.shape[1]
    return pl.pallas_call(
        _matmul_kernel,
        grid=(T // tm,),
        in_specs=[pl.BlockSpec((tm, D), lambda i: (i, 0)), pl.BlockSpec((D, N), lambda i: (0, 0))],
        out_specs=pl.BlockSpec((tm, N), lambda i: (i, 0)),
        out_shape=jax.ShapeDtypeStruct((T, N), F32),
        compiler_params=_cparams(("parallel",)),
        name="peer_query",
    )(h2, w_query)


_NO_POS = 2 ** 30


def _topk_rows(s, k, pos):
    vals, idxs = [], []
    for _ in range(k):
        m = jnp.max(s, axis=0, keepdims=True)
        i = jnp.min(jnp.where(s == m, pos, _NO_POS), axis=0, keepdims=True)
        vals.append(m)
        idxs.append(i)
        s = jnp.where(pos == i, -jnp.inf, s)
    return jnp.concatenate(vals, axis=0), jnp.concatenate(idxs, axis=0)


def _peer_route_kernel(q_ref, keys_ref, idx_ref, gate_ref, *, tt):
    K = PEER_TOPK
    half = PEER_QDIM // 2
    pos_k = lax.broadcasted_iota(jnp.int32, (PEER_NKEYS, tt), 0)
    row_k = lax.broadcasted_iota(jnp.int32, (K, tt), 0)
    row_8 = lax.broadcasted_iota(jnp.int32, (8, tt), 0)
    assert K == 16 and all((i < 4) or (i < 8 and j < 4) or (j == 0)
                           for i in range(K) for j in range(K) if (i + 1) * (j + 1) <= K)
    idx_rows, gate_rows = [], []
    for h in range(PEER_HEADS):
        tops = []
        for p in range(2):
            qh = q_ref[:, (2 * h + p) * half:(2 * h + p + 1) * half]
            sc = _dot_nt_hi(keys_ref[h, p], qh)
            tops.append(_topk_rows(sc, K, pos_k))
        (s0, i0), (s1, i1) = tops
        cb, pb, ib = [], [], []
        for i in range(4):
            cb.append(s0[i:i + 1] + s1)
            pb.append(i * K + row_k)
            ib.append(i0[i:i + 1] * PEER_NKEYS + i1)
        for j in range(4):
            cb.append(jnp.where(row_8 >= 4, s0[0:8] + s1[j:j + 1], -jnp.inf))
            pb.append(row_8 * K + j)
            ib.append(i0[0:8] * PEER_NKEYS + i1[j:j + 1])
        cb.append(s0[8:16] + s1[0:1])
        pb.append((row_8 + 8) * K)
        ib.append(i0[8:16] * PEER_NKEYS + i1[0:1])
        cand = jnp.concatenate(cb, axis=0)
        pos_c = jnp.concatenate(pb, axis=0)
        cidx = jnp.concatenate(ib, axis=0)
        best, bpos = _topk_rows(cand, K, pos_c)
        eidx = jnp.concatenate(
            [jnp.min(jnp.where(pos_c == bpos[r:r + 1], cidx, PEER_NKEYS * PEER_NKEYS),
                     axis=0, keepdims=True) for r in range(K)], axis=0)
        e = jnp.exp(best - best[0:1])
        g = e / jnp.sum(e, axis=0, keepdims=True)
        zero = jnp.zeros_like(g[0:1])
        for r in range(K):
            gate_rows += [zero, g[r:r + 1]]
        idx_rows.append(eidx)
    idx_ref[...] = jnp.concatenate(idx_rows, axis=0).T
    gate_ref[...] = jnp.concatenate(gate_rows, axis=0).T


def _peer_route(qry, sub_keys, tt=256):
    T, N = qry.shape
    PK = PEER_HEADS * PEER_TOPK
    return pl.pallas_call(
        functools.partial(_peer_route_kernel, tt=tt),
        grid=(T // tt,),
        in_specs=[pl.BlockSpec((tt, N), lambda i: (i, 0)),
                  pl.BlockSpec(sub_keys.shape, lambda i: (0, 0, 0, 0))],
        out_specs=[pl.BlockSpec((tt, PK), lambda i: (i, 0)), pl.BlockSpec((tt, 2 * PK), lambda i: (i, 0))],
        out_shape=[jax.ShapeDtypeStruct((T, PK), jnp.int32), jax.ShapeDtypeStruct((T, 2 * PK), F32)],
        compiler_params=_cparams(("parallel",)),
        name="peer_route",
    )(qry, sub_keys)


PEER_TOKENS_PER_STEP = 4
PEER_AHEAD = 2
PEER_TOKEN_GROUP = 16
PEER_BANKS = PEER_TOKEN_GROUP // PEER_TOKENS_PER_STEP
PEER_SLOTS = PEER_BANKS * PEER_TOKENS_PER_STEP


def _gelu(x):
    return 0.5 * x * (1.0 + lax.erf(x * (2.0 ** -0.5)))


def _pack_expert_table(u, v):
    def pack(x):
        E, D = x.shape
        b = lax.bitcast_convert_type(x.astype(BF16), jnp.uint16).astype(jnp.uint32)
        return (b[:, :D // 2] | (b[:, D // 2:] << 16)).reshape(E, D // (2 * LANES), LANES)
    return jnp.concatenate([pack(u), pack(v)], axis=1)


def _peer_expert_kernel(idx_ref, gate_ref, h_ref, y_ref, tab_ref, o_ref, buf, sem, *, tt, D):
    PK = PEER_HEADS * PEER_TOPK
    GT = PEER_TOKEN_GROUP
    HD = D // 2
    R = HD // LANES

    def issue(t, slot, k0=0, k1=PK):
        for k in range(k0, k1):
            e = idx_ref[t, k]
            pltpu.make_async_copy(tab_ref.at[e], buf.at[slot, :, k], sem.at[slot]).start(priority=k % 2)

    def wait(slot):
        pltpu.make_async_copy(buf.at[slot], buf.at[slot], sem.at[slot]).wait()

    TPS = PEER_TOKENS_PER_STEP

    AHEAD = PEER_AHEAD
    NSTEP = GT // TPS
    assert AHEAD < PEER_BANKS

    for t in range(AHEAD * TPS):
        issue(t, t)

    even = lax.broadcasted_iota(jnp.int32, (1, 2 * PK), 1) % 2 == 0

    def token(j, slot, hr, gg, prefetch):
        z = jnp.zeros((1, 2 * PK), F32)
        for r in range(R):
            ub = pltpu.bitcast(buf[slot, r], BF16)
            res = lax.dot_general(hr[r], ub, (((1,), (1,)), ((), ())), preferred_element_type=F32)
            z = z + jnp.where(even, res[j:j + 1], res[GT + j:GT + j + 1])
            prefetch(r)
        s = z + pltpu.roll(z, 1, 1)
        a = _gelu(s) * gg[j:j + 1]
        a2 = jnp.concatenate([pltpu.roll(a, 2 * PK - 1, 1), a], axis=0).astype(BF16)
        o2 = []
        for r in range(R):
            o2.append(jnp.dot(a2, pltpu.bitcast(buf[slot, R + r], BF16), preferred_element_type=F32))
            prefetch(R + r)
        return jnp.concatenate([o[0:1] for o in o2] + [o[1:2] for o in o2], axis=1)

    def group(t0, last):
        hg = h_ref[pl.ds(t0, GT), :]
        gg = gate_ref[pl.ds(t0, GT), :]
        hr = [jnp.concatenate([hg[:, r * LANES:(r + 1) * LANES],
                               hg[:, HD + r * LANES:HD + (r + 1) * LANES]], axis=0) for r in range(R)]
        outs = []
        for step in range(NSTEP):
            bank, nbank = step * TPS, ((step + AHEAD) % NSTEP) * TPS
            for i in range(TPS):
                wait(bank + i)
            for i in range(TPS):
                def prefetch(stage, i=i, step=step, nbank=nbank):
                    if not (last and step + AHEAD >= NSTEP):
                        per = PK // (2 * R)
                        issue(t0 + (step + AHEAD) * TPS + i, nbank + i, stage * per, (stage + 1) * per)
                outs.append(token(step * TPS + i, bank + i, hr, gg, prefetch))
        o_ref[pl.ds(t0, GT), :] = y_ref[pl.ds(t0, GT), :] + jnp.concatenate(outs, axis=0)

    def body(gi, c):
        group(pl.multiple_of(gi * GT, GT), False)
        return c

    lax.fori_loop(0, tt // GT - 1, body, 0)
    group(tt - GT, True)


def _peer_experts(idx, gate2, h2, y1, table, tt=256):
    T, D = y1.shape
    PK = PEER_HEADS * PEER_TOPK
    return pl.pallas_call(
        functools.partial(_peer_expert_kernel, tt=tt, D=D),
        grid=(T // tt,),
        in_specs=[
            pl.BlockSpec((tt, PK), lambda i: (i, 0), memory_space=pltpu.SMEM),
            pl.BlockSpec((tt, 2 * PK), lambda i: (i, 0)),
            pl.BlockSpec((tt, D), lambda i: (i, 0)),
            pl.BlockSpec((tt, D), lambda i: (i, 0)),
            pl.BlockSpec(memory_space=pl.ANY),
        ],
        out_specs=pl.BlockSpec((tt, D), lambda i: (i, 0)),
        out_shape=jax.ShapeDtypeStruct((T, D), F32),
        scratch_shapes=[pltpu.VMEM((PEER_SLOTS, table.shape[1], PK, LANES), jnp.uint32),
                        pltpu.SemaphoreType.DMA((PEER_SLOTS,))],
        compiler_params=pltpu.CompilerParams(dimension_semantics=("arbitrary",),
                                             vmem_limit_bytes=VMEM_LIMIT, disable_bounds_checks=True),
        name="peer_experts",
    )(idx, gate2, h2, y1, table)


def _layer(x, rel_bias, norm1_gain, w_in, conv_w, a_log, dt_bias, q_norm_gain, k_norm_gain,
           gdn_norm_gain, w_up_a, w_up_b, w_out, norm2_gain, peer_w_query, peer_sub_keys,
           peer_u, peer_v):
    B, S, D = x.shape
    T = B * S
    x2 = x.reshape(T, D)
    n_main = 3 * A_HEADS * A_HEAD_DIM + 4 * B_HEADS * B_KEY_DIM
    n_small = 2 * B_HEADS
    w_main = jnp.concatenate([w_in[:, :n_main], w_in[:, n_main + n_small:]], axis=1).astype(BF16)
    w_small = jnp.pad(w_in[:, n_main:n_main + n_small], ((0, 0), (0, LANES - n_small))).astype(BF16)
    proj, small = _in_proj(x2, norm1_gain[None, :], w_main, w_small)

    t0, t1, far = _bias_tiles(rel_bias)
    o_a = _moba(proj, q_norm_gain[None, :], k_norm_gain[None, :], t0, t1, far, B, S)

    pad8 = lambda v: jnp.pad(v, (B_HEADS, LANES - 2 * B_HEADS))[None, :]
    o_b = _gdn(proj, small, conv_w, pad8(a_log), pad8(dt_bias), gdn_norm_gain[None, :], B, S)

    merged = _merge(o_a, o_b, w_up_a.astype(BF16), w_up_b.astype(BF16), proj)
    y1, h2 = _out_proj(merged, w_out.astype(BF16), x2, norm2_gain[None, :])

    qry = _peer_query(h2, peer_w_query.astype(BF16))
    idx, gate2 = _peer_route(qry, peer_sub_keys)
    y = _peer_experts(idx, gate2, h2, y1, _pack_expert_table(peer_u, peer_v))
    return y.reshape(B, S, D)


def kernel(x, rel_bias, norm1_gain, w_in, conv_w, a_log, dt_bias, q_norm_gain, k_norm_gain,
           gdn_norm_gain, w_up_a, w_up_b, w_out, norm2_gain, peer_w_query, peer_sub_keys,
           peer_u, peer_v):
    depth = w_in.shape[0]
    for l in range(depth):
        x = _layer(x, rel_bias, norm1_gain[l], w_in[l], conv_w[l], a_log[l], dt_bias[l],
                   q_norm_gain[l], k_norm_gain[l], gdn_norm_gain[l], w_up_a[l], w_up_b[l],
                   w_out[l], norm2_gain[l], peer_w_query[l], peer_sub_keys[l], peer_u[l], peer_v[l])
    return x
```

```python
import functools
import math

import jax
import jax.numpy as jnp
import numpy as np
from jax import lax
from jax.experimental import pallas as pl
from jax.experimental.pallas import tpu as pltpu

F32 = jnp.float32
BF16 = jnp.bfloat16
HIGHEST = lax.Precision.HIGHEST

LANES = 128
EPS = 1e-6

A_HEADS = 8
A_HEAD_DIM = 128
MOBA_BLOCK = 256
MOBA_TOPK = 3
REL_BUCKETS = 32
REL_MAX_DIST = 128
B_HEADS = 8
B_KEY_DIM = 128
B_VAL_DIM = 128
CONV_WIDTH = 4
DELTA_CHUNK = 64
GDN_GROUP = 256
GDN_HEADS_PER_STEP = 2
PEER_HEADS = 8
PEER_NKEYS = 128
PEER_QDIM = 256
PEER_TOPK = 16

VMEM_LIMIT = 56 * 1024 * 1024


def _cparams(sem):
    return pltpu.CompilerParams(dimension_semantics=sem, vmem_limit_bytes=VMEM_LIMIT)


def _dot(a, b):
    return jnp.dot(a.astype(BF16), b.astype(BF16), preferred_element_type=F32)


def _dot_nt(a, b):
    return lax.dot_general(a.astype(BF16), b.astype(BF16), (((1,), (1,)), ((), ())),
                           preferred_element_type=F32)


def _dot_tn(a, b):
    return lax.dot_general(a.astype(BF16), b.astype(BF16), (((0,), (0,)), ((), ())),
                           preferred_element_type=F32)


def _dot_hi(a, b):
    return jnp.dot(a, b, preferred_element_type=F32, precision=HIGHEST)


def _dot_nt_hi(a, b):
    return lax.dot_general(a, b, (((1,), (1,)), ((), ())), preferred_element_type=F32,
                           precision=HIGHEST)


def _split2(a):
    hi = a.astype(BF16)
    return hi, (a - hi.astype(F32)).astype(BF16)


def _mm(a, b):
    return jnp.dot(a, b, preferred_element_type=F32)


def _dot3(a2, b2):
    (ah, al), (bh, bl) = a2, b2
    return _mm(ah, bh) + (_mm(ah, bl) + _mm(al, bh))


def _dot_exact01(m01, x):
    m = m01.astype(BF16)
    hi = x.astype(BF16)
    r1 = x - hi.astype(F32)
    mid = r1.astype(BF16)
    lo = (r1 - mid.astype(F32)).astype(BF16)
    return _mm(m, hi) + (_mm(m, mid) + _mm(m, lo))


def _dot_exact01_rhs(x, m01):
    m = m01.astype(BF16)
    hi = x.astype(BF16)
    r1 = x - hi.astype(F32)
    mid = r1.astype(BF16)
    lo = (r1 - mid.astype(F32)).astype(BF16)
    return _mm(hi, m) + (_mm(mid, m) + _mm(lo, m))


def _sigmoid(x):
    return 1.0 / (1.0 + jnp.exp(-x))


def _silu(x):
    return x * _sigmoid(x)


def _in_proj_kernel(x_ref, g_ref, w_ref, ws_ref, o_ref, os_ref, h_scr):
    @pl.when(pl.program_id(1) == 0)
    def _():
        x = x_ref[...]
        ms = jnp.mean(x * x, axis=-1, keepdims=True)
        h = (x * lax.rsqrt(ms + EPS) * g_ref[...]).astype(BF16)
        h_scr[...] = h
        os_ref[...] = jnp.dot(h, ws_ref[...], preferred_element_type=F32)

    o_ref[...] = jnp.dot(h_scr[...], w_ref[...], preferred_element_type=F32)


def _in_proj(x2, gain, w_main, w_small, tm=1024, tn=1024):
    T, D = x2.shape
    N = w_main.shape[1]
    return pl.pallas_call(
        _in_proj_kernel,
        grid=(T // tm, N // tn),
        in_specs=[
            pl.BlockSpec((tm, D), lambda i, j: (i, 0)),
            pl.BlockSpec((1, D), lambda i, j: (0, 0)),
            pl.BlockSpec((D, tn), lambda i, j: (0, j)),
            pl.BlockSpec((D, LANES), lambda i, j: (0, 0)),
        ],
        out_specs=[
            pl.BlockSpec((tm, tn), lambda i, j: (i, j)),
            pl.BlockSpec((tm, LANES), lambda i, j: (i, 0)),
        ],
        out_shape=[jax.ShapeDtypeStruct((T, N), F32), jax.ShapeDtypeStruct((T, LANES), F32)],
        scratch_shapes=[pltpu.VMEM((tm, D), BF16)],
        compiler_params=_cparams(("parallel", "arbitrary")),
        name="in_proj",
    )(x2, gain, w_main, w_small)


def _moba_kernel(q_ref, k_ref, v_ref, qg_ref, kg_ref, t0_ref, t1_ref, far_ref, o_ref, *, nb):
    BS = MOBA_BLOCK
    scale = A_HEAD_DIM ** -0.5
    q = q_ref[...]
    k = k_ref[...]
    qn = q * lax.rsqrt(jnp.mean(q * q, axis=-1, keepdims=True) + EPS) * qg_ref[...]
    kn = k * lax.rsqrt(jnp.mean(k * k, axis=-1, keepdims=True) + EPS) * kg_ref[...]
    kmean = jnp.concatenate(
        [jnp.mean(kn[j * BS:(j + 1) * BS], axis=0, keepdims=True) for j in range(nb)], axis=0)
    knb = kn.astype(BF16)
    vb = v_ref[...].astype(BF16)
    t0 = t0_ref[...]
    t1 = t1_ref[...]
    far = far_ref[0:1, :]
    rows = lax.broadcasted_iota(jnp.int32, (BS, BS), 0)
    cols = lax.broadcasted_iota(jnp.int32, (BS, BS), 1)
    causal = cols <= rows
    blk_iota = lax.broadcasted_iota(jnp.int32, (BS, nb), 1)
    neg = -jnp.inf
    for qi in range(nb):
        qblk = qn[qi * BS:(qi + 1) * BS]
        sel = jnp.zeros((BS, nb), dtype=jnp.bool_)
        if qi > 0:
            gate = _dot_nt_hi(qblk, kmean)
            gate = jnp.where(blk_iota < qi, gate, neg)
            for _ in range(min(MOBA_TOPK, nb)):
                m = jnp.max(gate, axis=1, keepdims=True)
                idx = jnp.min(jnp.where(gate == m, blk_iota, nb), axis=1, keepdims=True)
                hit = (blk_iota == idx) & (m > neg)
                sel = sel | hit
                gate = jnp.where(blk_iota == idx, neg, gate)
        self32 = sel.astype(F32)
        s_all = _dot_nt(qblk, knb[:(qi + 1) * BS]) * scale
        blocks = []
        for j in range(qi + 1):
            s = s_all[:, j * BS:(j + 1) * BS]
            if j == qi:
                s = jnp.where(causal, s + t0, neg)
            else:
                bias = t1 if j == qi - 1 else far
                picked = self32[:, j:j + 1] > 0.5
                s = jnp.where(picked, s + bias, neg)
            blocks.append(s)
        smax = blocks[-1]
        for s in blocks[:-1]:
            smax = jnp.maximum(smax, s)
        m = jnp.max(smax, axis=1, keepdims=True)
        psum = jnp.zeros((BS, BS), F32)
        acc = jnp.zeros((BS, A_HEAD_DIM), F32)
        for j, s in enumerate(blocks):
            p = jnp.exp(s - m)
            psum = psum + p
            acc = acc + _dot(p, vb[j * BS:(j + 1) * BS])
        l = jnp.sum(psum, axis=1, keepdims=True)
        o_ref[qi * BS:(qi + 1) * BS, :] = (acc / l).astype(o_ref.dtype)


def _moba(proj, q_gain, k_gain, t0, t1, far, B, S):
    H = A_HEADS
    nb = S // MOBA_BLOCK
    dh = A_HEAD_DIM
    return pl.pallas_call(
        functools.partial(_moba_kernel, nb=nb),
        grid=(B, H),
        in_specs=[
            pl.BlockSpec((S, dh), lambda b, h: (b, h)),
            pl.BlockSpec((S, dh), lambda b, h: (b, H + h)),
            pl.BlockSpec((S, dh), lambda b, h: (b, 2 * H + h)),
            pl.BlockSpec((1, dh), lambda b, h: (0, 0)),
            pl.BlockSpec((1, dh), lambda b, h: (0, 0)),
            pl.BlockSpec((None, MOBA_BLOCK, MOBA_BLOCK), lambda b, h: (h, 0, 0)),
            pl.BlockSpec((None, MOBA_BLOCK, MOBA_BLOCK), lambda b, h: (h, 0, 0)),
            pl.BlockSpec((None, 8, MOBA_BLOCK), lambda b, h: (h, 0, 0)),
        ],
        out_specs=pl.BlockSpec((S, dh), lambda b, h: (b, h)),
        out_shape=jax.ShapeDtypeStruct((B * S, H * dh), BF16),
        compiler_params=_cparams(("parallel", "parallel")),
        name="moba",
    )(proj, proj, proj, q_gain, k_gain, t0, t1, far)


def _t5_bucket(rel):
    n = jnp.maximum(-rel, 0)
    max_exact = REL_BUCKETS // 2
    nf = jnp.maximum(n, 1).astype(F32)
    large = max_exact + (jnp.log(nf / max_exact) / math.log(REL_MAX_DIST / max_exact)
                         * (REL_BUCKETS - max_exact)).astype(jnp.int32)
    large = jnp.minimum(large, REL_BUCKETS - 1)
    return jnp.where(n < max_exact, n, large)


def _bias_tiles(rel_bias):
    assert MOBA_BLOCK + 1 >= REL_MAX_DIST
    offs = jnp.arange(MOBA_BLOCK)
    rel0 = offs[None, :] - offs[:, None]
    bias_hb = rel_bias.T.astype(F32)

    def tile(rel):
        onehot = jax.nn.one_hot(_t5_bucket(rel), REL_BUCKETS, dtype=F32)
        return jnp.einsum('hb,qkb->hqk', bias_hb, onehot, precision=HIGHEST)

    t0 = tile(rel0)
    t1 = tile(rel0 - MOBA_BLOCK)
    far = jnp.broadcast_to(bias_hb[:, REL_BUCKETS - 1][:, None, None], (A_HEADS, 8, MOBA_BLOCK))
    return t0, t1, far


def _gdn_kernel(q_ref, k_ref, v_ref, z_ref, sm_ref, cwq_ref, cwk_ref, cwv_ref, alog_ref, dtb_ref,
                ng_ref, o_ref, qs, ks, vs, bs, gs, *, S):
    C = DELTA_CHUNK
    G = GDN_GROUP
    dk = B_KEY_DIM
    HP = GDN_HEADS_PER_STEP
    NG = S // G
    sel_r = lax.broadcasted_iota(jnp.int32, (LANES, LANES), 0)

    def conv_silu(x_ref, w_ref, r0, first):
        w = w_ref[...]
        cur = x_ref[pl.ds(r0, G), :]
        above = x_ref[pl.ds(pl.multiple_of(jnp.maximum(r0 - 8, 0), 8), 8), :]
        xcat = jnp.concatenate([jnp.where(first, 0.0, above), cur], axis=0)
        y = cur * w[CONV_WIDTH - 1:CONV_WIDTH, :]
        for d in range(1, CONV_WIDTH):
            y = y + pltpu.roll(xcat, d, 0)[8:] * w[CONV_WIDTH - 1 - d:CONV_WIDTH - d, :]
        return _silu(y)

    def l2n(x):
        parts = [x[:, i * dk:(i + 1) * dk] for i in range(HP)]
        return jnp.concatenate(
            [p * lax.rsqrt(jnp.sum(p * p, axis=-1, keepdims=True) + EPS) for p in parts], axis=1)

    def prepare(gi, slot):
        r0 = pl.multiple_of(gi * G, G)
        first = gi == 0
        qs[slot] = l2n(conv_silu(q_ref, cwq_ref, r0, first)) * (dk ** -0.5)
        yield
        ks[slot] = l2n(conv_silu(k_ref, cwk_ref, r0, first))
        yield
        vs[slot] = conv_silu(v_ref, cwv_ref, r0, first)
        yield
        sm = sm_ref[pl.ds(r0, G), :]
        beta_all = _sigmoid(sm)
        z = sm + dtb_ref[...]
        softplus = jnp.maximum(z, 0.0) + jnp.log(1.0 + jnp.exp(-jnp.abs(z)))
        g_all = -jnp.exp(alog_ref[...]) * softplus
        for i in range(HP):
            hh = pl.program_id(1) * HP + i
            bs[slot, i] = _dot_exact01_rhs(beta_all, sel_r == hh)
            gs[slot, i] = _dot_exact01_rhs(g_all, sel_r == hh + B_HEADS)
            yield

    for _ in prepare(jnp.int32(0), 0):
        pass

    ri = lax.broadcasted_iota(jnp.int32, (G, G), 0)
    ci = lax.broadcasted_iota(jnp.int32, (G, G), 1)
    same = (ri // C) == (ci // C)
    strict = same & (ri > ci)
    causal = same & (ri >= ci)
    tri = causal
    strict_f = strict.astype(F32)
    eye = (ri == ci).astype(F32)
    ng = ng_ref[...]

    def group(gi, states):
        r0 = pl.multiple_of(gi * G, G)
        slot = gi % 2
        gens = [head_group(r0, slot, i, states[i]) for i in range(HP)]
        gens.append(prepare(jnp.minimum(gi + 1, NG - 1), 1 - slot))
        done = {}
        while len(done) < len(gens):
            for i in range(len(gens)):
                if i not in done:
                    try:
                        next(gens[i])
                    except StopIteration as stop:
                        done[i] = stop.value
        return tuple(done[i] for i in range(HP))

    def head_group(r0, slot, i, state):
        lanes = slice(i * dk, (i + 1) * dk)
        qg = qs[slot, :, lanes]
        kg = ks[slot, :, lanes]
        vg = vs[slot, :, lanes]
        bg = bs[slot, i]
        gg = gs[slot, i]
        gmat = jnp.concatenate([gg, gg], axis=1)
        dmat = _dot_exact01(tri, gmat * strict_f)
        gcum = _dot_exact01(tri, gg)
        yield
        decay = jnp.where(causal, jnp.exp(jnp.where(causal, dmat, 0.0)), 0.0)
        kb = kg * bg
        vb = vg * bg
        kk = _dot_nt(kb, kg)
        yield
        a = -(kk * decay) * strict_f
        t = eye + a
        p2 = _split2(a)
        for _ in range(int(math.log2(C)) - 1):
            p2 = _split2(_dot3(p2, p2))
            yield
            t = t + _dot3(_split2(t), p2)
            yield
        eg = jnp.exp(gcum)
        tb = t.astype(BF16)
        u = _mm(tb, vb.astype(BF16))
        w = _mm(tb, (kb * eg).astype(BF16))
        attn = _dot_nt(qg, kg) * decay
        yield
        qd = qg * eg
        outs = []
        vnews = []
        for c in range(G // C):
            sl = slice(c * C, (c + 1) * C)
            glast = gcum[(c + 1) * C - 1:(c + 1) * C, :]
            kd = kg[sl] * jnp.exp(glast - gcum[sl])
            sb = state.astype(BF16)
            v_new = u[sl] - _mm(w[sl].astype(BF16), sb)
            outs.append(_mm(qd[sl].astype(BF16), sb))
            vnews.append(v_new)
            yield
            state = state * jnp.exp(glast)[:, 0:1] + _dot_tn(kd, v_new)
            yield
        o = jnp.concatenate(outs, axis=0) + _dot(attn, jnp.concatenate(vnews, axis=0))
        on = o * lax.rsqrt(jnp.mean(o * o, axis=-1, keepdims=True) + EPS) * ng
        o_ref[pl.ds(r0, G), lanes] = (on * _silu(z_ref[pl.ds(r0, G), lanes])).astype(o_ref.dtype)
        return state

    lax.fori_loop(0, S // G, group, tuple(jnp.zeros((dk, B_VAL_DIM), F32) for _ in range(HP)))


def _gdn(proj, small, conv_w, alog_pad, dtb_pad, norm_gain, B, S):
    HP = GDN_HEADS_PER_STEP
    NH = B_HEADS // HP
    W = HP * B_KEY_DIM
    qoff = 3 * A_HEADS // HP
    blk = lambda off: pl.BlockSpec((S, W), lambda b, h: (b, off + h))
    cw = lambda off: pl.BlockSpec((CONV_WIDTH, W), lambda b, h: (0, off + h))
    one = pl.BlockSpec((1, LANES), lambda b, h: (0, 0))
    return pl.pallas_call(
        functools.partial(_gdn_kernel, S=S),
        grid=(B, NH),
        in_specs=[blk(qoff), blk(qoff + NH), blk(qoff + 2 * NH), blk(qoff + 3 * NH),
                  pl.BlockSpec((S, LANES), lambda b, h: (b, 0)),
                  cw(0), cw(NH), cw(2 * NH),
                  one, one, one],
        out_specs=pl.BlockSpec((S, W), lambda b, h: (b, h)),
        out_shape=jax.ShapeDtypeStruct((B * S, B_HEADS * B_VAL_DIM), BF16),
        scratch_shapes=([pltpu.VMEM((2, GDN_GROUP, W), F32)] * 3
                        + [pltpu.VMEM((2, HP, GDN_GROUP, LANES), F32)] * 2),
        compiler_params=_cparams(("parallel", "parallel")),
        name="gdn",
    )(proj, proj, proj, proj, small, conv_w, conv_w, conv_w, alog_pad, dtb_pad, norm_gain)


def _merge_kernel(oa_ref, ob_ref, wa_ref, wb_ref, ga_ref, gb_ref, o_ref):
    a = jnp.dot(oa_ref[...], wa_ref[...], preferred_element_type=F32)
    b = jnp.dot(ob_ref[...], wb_ref[...], preferred_element_type=F32)
    o_ref[...] = (_sigmoid(ga_ref[...]) * a + _sigmoid(gb_ref[...]) * b).astype(o_ref.dtype)


def _merge(o_a, o_b, w_up_a, w_up_b, proj, tm=512, tn=1024):
    T, K = o_a.shape
    D = w_up_a.shape[1]
    ga_off = (3 * A_HEADS + 4 * B_HEADS) * LANES // tn
    gb_off = ga_off + D // tn
    return pl.pallas_call(
        _merge_kernel,
        grid=(T // tm, D // tn),
        in_specs=[
            pl.BlockSpec((tm, K), lambda i, j: (i, 0)),
            pl.BlockSpec((tm, K), lambda i, j: (i, 0)),
            pl.BlockSpec((K, tn), lambda i, j: (0, j)),
            pl.BlockSpec((K, tn), lambda i, j: (0, j)),
            pl.BlockSpec((tm, tn), lambda i, j: (i, ga_off + j)),
            pl.BlockSpec((tm, tn), lambda i, j: (i, gb_off + j)),
        ],
        out_specs=pl.BlockSpec((tm, tn), lambda i, j: (i, j)),
        out_shape=jax.ShapeDtypeStruct((T, D), BF16),
        compiler_params=_cparams(("parallel", "parallel")),
        name="merge",
    )(o_a, o_b, w_up_a, w_up_b, proj, proj)


def _out_proj_kernel(m_ref, w_ref, x_ref, g_ref, y_ref, h_ref):
    y = x_ref[...] + jnp.dot(m_ref[...], w_ref[...], preferred_element_type=F32)
    y_ref[...] = y
    ms = jnp.mean(y * y, axis=-1, keepdims=True)
    h_ref[...] = (y * lax.rsqrt(ms + EPS) * g_ref[...]).astype(h_ref.dtype)


def _out_proj(merged, w_out, x2, gain2, tm=512):
    T, D = x2.shape
    return pl.pallas_call(
        _out_proj_kernel,
        grid=(T // tm,),
        in_specs=[
            pl.BlockSpec((tm, D), lambda i: (i, 0)),
            pl.BlockSpec((D, D), lambda i: (0, 0)),
            pl.BlockSpec((tm, D), lambda i: (i, 0)),
            pl.BlockSpec((1, D), lambda i: (0, 0)),
        ],
        out_specs=[pl.BlockSpec((tm, D), lambda i: (i, 0)), pl.BlockSpec((tm, D), lambda i: (i, 0))],
        out_shape=[jax.ShapeDtypeStruct((T, D), F32), jax.ShapeDtypeStruct((T, D), BF16)],
        compiler_params=_cparams(("parallel",)),
        name="out_proj",
    )(merged, w_out, x2, gain2)


def _matmul_kernel(a_ref, b_ref, o_ref):
    o_ref[...] = jnp.dot(a_ref[...], b_ref[...], preferred_element_type=F32)


def _peer_query(h2, w_query, tm=512):
    T, D = h2.shape
    N = w_query.shape[1]
    return pl.pallas_call(
        _matmul_kernel,
        grid=(T // tm,),
        in_specs=[pl.BlockSpec((tm, D), lambda i: (i, 0)), pl.BlockSpec((D, N), lambda i: (0, 0))],
        out_specs=pl.BlockSpec((tm, N), lambda i: (i, 0)),
        out_shape=jax.ShapeDtypeStruct((T, N), F32),
        compiler_params=_cparams(("parallel",)),
        name="peer_query",
    )(h2, w_query)


_NO_POS = 2 ** 30


def _topk_rows(s, k, pos):
    vals, idxs = [], []
    for _ in range(k):
        m = jnp.max(s, axis=0, keepdims=True)
        i = jnp.min(jnp.where(s == m, pos, _NO_POS), axis=0, keepdims=True)
        vals.append(m)
        idxs.append(i)
        s = jnp.where(pos == i, -jnp.inf, s)
    return jnp.concatenate(vals, axis=0), jnp.concatenate(idxs, axis=0)


def _peer_route_kernel(q_ref, keys_ref, idx_ref, gate_ref, *, tt):
    K = PEER_TOPK
    half = PEER_QDIM // 2
    pos_k = lax.broadcasted_iota(jnp.int32, (PEER_NKEYS, tt), 0)
    row_k = lax.broadcasted_iota(jnp.int32, (K, tt), 0)
    row_8 = lax.broadcasted_iota(jnp.int32, (8, tt), 0)
    assert K == 16 and all((i < 4) or (i < 8 and j < 4) or (j == 0)
                           for i in range(K) for j in range(K) if (i + 1) * (j + 1) <= K)
    idx_rows, gate_rows = [], []
    for h in range(PEER_HEADS):
        tops = []
        for p in range(2):
            qh = q_ref[:, (2 * h + p) * half:(2 * h + p + 1) * half]
            sc = _dot_nt_hi(keys_ref[h, p], qh)
            tops.append(_topk_rows(sc, K, pos_k))
        (s0, i0), (s1, i1) = tops
        cb, pb, ib = [], [], []
        for i in range(4):
            cb.append(s0[i:i + 1] + s1)
            pb.append(i * K + row_k)
            ib.append(i0[i:i + 1] * PEER_NKEYS + i1)
        for j in range(4):
            cb.append(jnp.where(row_8 >= 4, s0[0:8] + s1[j:j + 1], -jnp.inf))
            pb.append(row_8 * K + j)
            ib.append(i0[0:8] * PEER_NKEYS + i1[j:j + 1])
        cb.append(s0[8:16] + s1[0:1])
        pb.append((row_8 + 8) * K)
        ib.append(i0[8:16] * PEER_NKEYS + i1[0:1])
        cand = jnp.concatenate(cb, axis=0)
        pos_c = jnp.concatenate(pb, axis=0)
        cidx = jnp.concatenate(ib, axis=0)
        best, bpos = _topk_rows(cand, K, pos_c)
        eidx = jnp.concatenate(
            [jnp.min(jnp.where(pos_c == bpos[r:r + 1], cidx, PEER_NKEYS * PEER_NKEYS),
                     axis=0, keepdims=True) for r in range(K)], axis=0)
        e = jnp.exp(best - best[0:1])
        g = e / jnp.sum(e, axis=0, keepdims=True)
        zero = jnp.zeros_like(g[0:1])
        for r in range(K):
            gate_rows += [zero, g[r:r + 1]]
        idx_rows.append(eidx)
    idx_ref[...] = jnp.concatenate(idx_rows, axis=0).T
    gate_ref[...] = jnp.concatenate(gate_rows, axis=0).T


def _peer_route(qry, sub_keys, tt=256):
    T, N = qry.shape
    PK = PEER_HEADS * PEER_TOPK
    return pl.pallas_call(
        functools.partial(_peer_route_kernel, tt=tt),
        grid=(T // tt,),
        in_specs=[pl.BlockSpec((tt, N), lambda i: (i, 0)),
                  pl.BlockSpec(sub_keys.shape, lambda i: (0, 0, 0, 0))],
        out_specs=[pl.BlockSpec((tt, PK), lambda i: (i, 0)), pl.BlockSpec((tt, 2 * PK), lambda i: (i, 0))],
        out_shape=[jax.ShapeDtypeStruct((T, PK), jnp.int32), jax.ShapeDtypeStruct((T, 2 * PK), F32)],
        compiler_params=_cparams(("parallel",)),
        name="peer_route",
    )(qry, sub_keys)


PEER_TOKENS_PER_STEP = 4
PEER_AHEAD = 2
PEER_TOKEN_GROUP = 16
PEER_BANKS = PEER_TOKEN_GROUP // PEER_TOKENS_PER_STEP
PEER_SLOTS = PEER_BANKS * PEER_TOKENS_PER_STEP


def _gelu(x):
    return 0.5 * x * (1.0 + lax.erf(x * (2.0 ** -0.5)))


def _pack_expert_table(u, v):
    def pack(x):
        E, D = x.shape
        b = lax.bitcast_convert_type(x.astype(BF16), jnp.uint16).astype(jnp.uint32)
        return (b[:, :D // 2] | (b[:, D // 2:] << 16)).reshape(E, D // (2 * LANES), LANES)
    return jnp.concatenate([pack(u), pack(v)], axis=1)


def _peer_expert_kernel(idx_ref, gate_ref, h_ref, y_ref, tab_ref, o_ref, buf, sem, *, tt, D):
    PK = PEER_HEADS * PEER_TOPK
    GT = PEER_TOKEN_GROUP
    HD = D // 2
    R = HD // LANES

    def issue(t, slot, k0=0, k1=PK):
        for k in range(k0, k1):
            e = idx_ref[t, k]
            pltpu.make_async_copy(tab_ref.at[e], buf.at[slot, :, k], sem.at[slot]).start(priority=k % 2)

    def wait(slot):
        pltpu.make_async_copy(buf.at[slot], buf.at[slot], sem.at[slot]).wait()

    TPS = PEER_TOKENS_PER_STEP

    AHEAD = PEER_AHEAD
    NSTEP = GT // TPS
    assert AHEAD < PEER_BANKS

    for t in range(AHEAD * TPS):
        issue(t, t)

    even = lax.broadcasted_iota(jnp.int32, (1, 2 * PK), 1) % 2 == 0

    def tokens(j0, bank, hr, gg, prefetch):
        zs = [jnp.zeros((1, 2 * PK), F32) for _ in range(TPS)]
        for r in range(R):
            for i in range(TPS):
                j = j0 + i
                ub = pltpu.bitcast(buf[bank + i, r], BF16)
                res = lax.dot_general(hr[r], ub, (((1,), (1,)), ((), ())), preferred_element_type=F32)
                zs[i] = zs[i] + jnp.where(even, res[j:j + 1], res[GT + j:GT + j + 1])
                prefetch(i, r)
        a2s = []
        for i in range(TPS):
            j = j0 + i
            s = zs[i] + pltpu.roll(zs[i], 1, 1)
            a = _gelu(s) * gg[j:j + 1]
            a2s.append(jnp.concatenate([pltpu.roll(a, 2 * PK - 1, 1), a], axis=0).astype(BF16))
        o2 = [[] for _ in range(TPS)]
        for r in range(R):
            for i in range(TPS):
                o2[i].append(jnp.dot(a2s[i], pltpu.bitcast(buf[bank + i, R + r], BF16),
                                     preferred_element_type=F32))
                prefetch(i, R + r)
        return [jnp.concatenate([o[0:1] for o in o2[i]] + [o[1:2] for o in o2[i]], axis=1)
                for i in range(TPS)]

    def group(t0, last):
        hg = h_ref[pl.ds(t0, GT), :]
        gg = gate_ref[pl.ds(t0, GT), :]
        hr = [jnp.concatenate([hg[:, r * LANES:(r + 1) * LANES],
                               hg[:, HD + r * LANES:HD + (r + 1) * LANES]], axis=0) for r in range(R)]
        outs = []
        for step in range(NSTEP):
            bank, nbank = step * TPS, ((step + AHEAD) % NSTEP) * TPS
            for i in range(TPS):
                wait(bank + i)

            def prefetch(i, stage, step=step, nbank=nbank):
                if not (last and step + AHEAD >= NSTEP):
                    per = PK // (2 * R)
                    issue(t0 + (step + AHEAD) * TPS + i, nbank + i, stage * per, (stage + 1) * per)
            outs += tokens(step * TPS, bank, hr, gg, prefetch)
        o_ref[pl.ds(t0, GT), :] = y_ref[pl.ds(t0, GT), :] + jnp.concatenate(outs, axis=0)

    def body(gi, c):
        group(pl.multiple_of(gi * GT, GT), False)
        return c

    lax.fori_loop(0, tt // GT - 1, body, 0)
    group(tt - GT, True)


def _peer_experts(idx, gate2, h2, y1, table, tt=512):
    T, D = y1.shape
    PK = PEER_HEADS * PEER_TOPK
    return pl.pallas_call(
        functools.partial(_peer_expert_kernel, tt=tt, D=D),
        grid=(T // tt,),
        in_specs=[
            pl.BlockSpec((tt, PK), lambda i: (i, 0), memory_space=pltpu.SMEM),
            pl.BlockSpec((tt, 2 * PK), lambda i: (i, 0)),
            pl.BlockSpec((tt, D), lambda i: (i, 0)),
            pl.BlockSpec((tt, D), lambda i: (i, 0)),
            pl.BlockSpec(memory_space=pl.ANY),
        ],
        out_specs=pl.BlockSpec((tt, D), lambda i: (i, 0)),
        out_shape=jax.ShapeDtypeStruct((T, D), F32),
        scratch_shapes=[pltpu.VMEM((PEER_SLOTS, table.shape[1], PK, LANES), jnp.uint32),
                        pltpu.SemaphoreType.DMA((PEER_SLOTS,))],
        compiler_params=pltpu.CompilerParams(dimension_semantics=("arbitrary",),
                                             vmem_limit_bytes=VMEM_LIMIT, disable_bounds_checks=True),
        name="peer_experts",
    )(idx, gate2, h2, y1, table)


def _layer(x, rel_bias, norm1_gain, w_in, conv_w, a_log, dt_bias, q_norm_gain, k_norm_gain,
           gdn_norm_gain, w_up_a, w_up_b, w_out, norm2_gain, peer_w_query, peer_sub_keys,
           peer_u, peer_v):
    B, S, D = x.shape
    T = B * S
    x2 = x.reshape(T, D)
    n_main = 3 * A_HEADS * A_HEAD_DIM + 4 * B_HEADS * B_KEY_DIM
    n_small = 2 * B_HEADS
    w_main = jnp.concatenate([w_in[:, :n_main], w_in[:, n_main + n_small:]], axis=1).astype(BF16)
    w_small = jnp.pad(w_in[:, n_main:n_main + n_small], ((0, 0), (0, LANES - n_small))).astype(BF16)
    proj, small = _in_proj(x2, norm1_gain[None, :], w_main, w_small)

    t0, t1, far = _bias_tiles(rel_bias)
    o_a = _moba(proj, q_norm_gain[None, :], k_norm_gain[None, :], t0, t1, far, B, S)

    pad8 = lambda v: jnp.pad(v, (B_HEADS, LANES - 2 * B_HEADS))[None, :]
    o_b = _gdn(proj, small, conv_w, pad8(a_log), pad8(dt_bias), gdn_norm_gain[None, :], B, S)

    merged = _merge(o_a, o_b, w_up_a.astype(BF16), w_up_b.astype(BF16), proj)
    y1, h2 = _out_proj(merged, w_out.astype(BF16), x2, norm2_gain[None, :])

    qry = _peer_query(h2, peer_w_query.astype(BF16))
    idx, gate2 = _peer_route(qry, peer_sub_keys)
    y = _peer_experts(idx, gate2, h2, y1, _pack_expert_table(peer_u, peer_v))
    return y.reshape(B, S, D)


def kernel(x, rel_bias, norm1_gain, w_in, conv_w, a_log, dt_bias, q_norm_gain, k_norm_gain,
           gdn_norm_gain, w_up_a, w_up_b, w_out, norm2_gain, peer_w_query, peer_sub_keys,
           peer_u, peer_v):
    depth = w_in.shape[0]
    for l in range(depth):
        x = _layer(x, rel_bias, norm1_gain[l], w_in[l], conv_w[l], a_log[l], dt_bias[l],
                   q_norm_gain[l], k_norm_gain[l], gdn_norm_gain[l], w_up_a[l], w_up_b[l],
                   w_out[l], norm2_gain[l], peer_w_query[l], peer_sub_keys[l], peer_u[l], peer_v[l])
    return x
```

```python
import functools
import math

import jax
import jax.numpy as jnp
import numpy as np
from jax import lax
from jax.experimental import pallas as pl
from jax.experimental.pallas import tpu as pltpu

F32 = jnp.float32
BF16 = jnp.bfloat16
HIGHEST = lax.Precision.HIGHEST

LANES = 128
EPS = 1e-6

A_HEADS = 8
A_HEAD_DIM = 128
MOBA_BLOCK = 256
MOBA_TOPK = 3
REL_BUCKETS = 32
REL_MAX_DIST = 128
B_HEADS = 8
B_KEY_DIM = 128
B_VAL_DIM = 128
CONV_WIDTH = 4
DELTA_CHUNK = 64
GDN_GROUP = 256
GDN_HEADS_PER_STEP = 4
PEER_HEADS = 8
PEER_NKEYS = 128
PEER_QDIM = 256
PEER_TOPK = 16

VMEM_LIMIT = 56 * 1024 * 1024


def _cparams(sem):
    return pltpu.CompilerParams(dimension_semantics=sem, vmem_limit_bytes=VMEM_LIMIT)


def _dot(a, b):
    return jnp.dot(a.astype(BF16), b.astype(BF16), preferred_element_type=F32)


def _dot_nt(a, b):
    return lax.dot_general(a.astype(BF16), b.astype(BF16), (((1,), (1,)), ((), ())),
                           preferred_element_type=F32)


def _dot_tn(a, b):
    return lax.dot_general(a.astype(BF16), b.astype(BF16), (((0,), (0,)), ((), ())),
                           preferred_element_type=F32)


def _dot_hi(a, b):
    return jnp.dot(a, b, preferred_element_type=F32, precision=HIGHEST)


def _dot_nt_hi(a, b):
    return lax.dot_general(a, b, (((1,), (1,)), ((), ())), preferred_element_type=F32,
                           precision=HIGHEST)


def _split2(a):
    hi = a.astype(BF16)
    return hi, (a - hi.astype(F32)).astype(BF16)


def _mm(a, b):
    return jnp.dot(a, b, preferred_element_type=F32)


def _dot3(a2, b2):
    (ah, al), (bh, bl) = a2, b2
    return _mm(ah, bh) + (_mm(ah, bl) + _mm(al, bh))


def _dot_exact01(m01, x):
    m = m01.astype(BF16)
    hi = x.astype(BF16)
    r1 = x - hi.astype(F32)
    mid = r1.astype(BF16)
    lo = (r1 - mid.astype(F32)).astype(BF16)
    return _mm(m, hi) + (_mm(m, mid) + _mm(m, lo))


def _dot_exact01_rhs(x, m01):
    m = m01.astype(BF16)
    hi = x.astype(BF16)
    r1 = x - hi.astype(F32)
    mid = r1.astype(BF16)
    lo = (r1 - mid.astype(F32)).astype(BF16)
    return _mm(hi, m) + (_mm(mid, m) + _mm(lo, m))


def _sigmoid(x):
    return 1.0 / (1.0 + jnp.exp(-x))


def _silu(x):
    return x * _sigmoid(x)


def _in_proj_kernel(x_ref, g_ref, w_ref, ws_ref, o_ref, os_ref, h_scr):
    @pl.when(pl.program_id(1) == 0)
    def _():
        x = x_ref[...]
        ms = jnp.mean(x * x, axis=-1, keepdims=True)
        h = (x * lax.rsqrt(ms + EPS) * g_ref[...]).astype(BF16)
        h_scr[...] = h
        os_ref[...] = jnp.dot(h, ws_ref[...], preferred_element_type=F32)

    o_ref[...] = jnp.dot(h_scr[...], w_ref[...], preferred_element_type=F32)


def _in_proj(x2, gain, w_main, w_small, tm=1024, tn=1024):
    T, D = x2.shape
    N = w_main.shape[1]
    return pl.pallas_call(
        _in_proj_kernel,
        grid=(T // tm, N // tn),
        in_specs=[
            pl.BlockSpec((tm, D), lambda i, j: (i, 0)),
            pl.BlockSpec((1, D), lambda i, j: (0, 0)),
            pl.BlockSpec((D, tn), lambda i, j: (0, j)),
            pl.BlockSpec((D, LANES), lambda i, j: (0, 0)),
        ],
        out_specs=[
            pl.BlockSpec((tm, tn), lambda i, j: (i, j)),
            pl.BlockSpec((tm, LANES), lambda i, j: (i, 0)),
        ],
        out_shape=[jax.ShapeDtypeStruct((T, N), F32), jax.ShapeDtypeStruct((T, LANES), F32)],
        scratch_shapes=[pltpu.VMEM((tm, D), BF16)],
        compiler_params=_cparams(("parallel", "arbitrary")),
        name="in_proj",
    )(x2, gain, w_main, w_small)


def _moba_kernel(q_ref, k_ref, v_ref, qg_ref, kg_ref, t0_ref, t1_ref, far_ref, o_ref, *, nb):
    BS = MOBA_BLOCK
    scale = A_HEAD_DIM ** -0.5
    q = q_ref[...]
    k = k_ref[...]
    qn = q * lax.rsqrt(jnp.mean(q * q, axis=-1, keepdims=True) + EPS) * qg_ref[...]
    kn = k * lax.rsqrt(jnp.mean(k * k, axis=-1, keepdims=True) + EPS) * kg_ref[...]
    kmean = jnp.concatenate(
        [jnp.mean(kn[j * BS:(j + 1) * BS], axis=0, keepdims=True) for j in range(nb)], axis=0)
    knb = kn.astype(BF16)
    vb = v_ref[...].astype(BF16)
    t0 = t0_ref[...]
    t1 = t1_ref[...]
    far = far_ref[0:1, :]
    rows = lax.broadcasted_iota(jnp.int32, (BS, BS), 0)
    cols = lax.broadcasted_iota(jnp.int32, (BS, BS), 1)
    causal = cols <= rows
    blk_iota = lax.broadcasted_iota(jnp.int32, (nb, BS), 0)
    neg = -jnp.inf
    for qi in range(nb):
        qblk = qn[qi * BS:(qi + 1) * BS]
        if qi > 0:
            gate = _dot_nt_hi(kmean, qblk)
            gate = jnp.where(blk_iota < qi, gate, neg)
            sel = jnp.zeros((nb, BS), F32)
            for _ in range(min(MOBA_TOPK, nb)):
                m = jnp.max(gate, axis=0, keepdims=True)
                idx = jnp.min(jnp.where(gate == m, blk_iota, nb), axis=0, keepdims=True)
                sel = jnp.where((blk_iota == idx) & (m > neg), 1.0, sel)
                gate = jnp.where(blk_iota == idx, neg, gate)
            self32 = jnp.concatenate([sel, jnp.zeros((LANES - nb, BS), F32)], axis=0).T
        else:
            self32 = jnp.zeros((BS, LANES), F32)
        s_all = _dot_nt(qblk, knb[:(qi + 1) * BS]) * scale
        blocks = []
        for j in range(qi + 1):
            s = s_all[:, j * BS:(j + 1) * BS]
            if j == qi:
                s = jnp.where(causal, s + t0, neg)
            else:
                bias = t1 if j == qi - 1 else far
                picked = self32[:, j:j + 1] > 0.5
                s = jnp.where(picked, s + bias, neg)
            blocks.append(s)
        smax = blocks[-1]
        for s in blocks[:-1]:
            smax = jnp.maximum(smax, s)
        m = jnp.max(smax, axis=1, keepdims=True)
        psum = jnp.zeros((BS, BS), F32)
        acc = jnp.zeros((BS, A_HEAD_DIM), F32)
        for j, s in enumerate(blocks):
            p = jnp.exp(s - m)
            psum = psum + p
            acc = acc + _dot(p, vb[j * BS:(j + 1) * BS])
        l = jnp.sum(psum, axis=1, keepdims=True)
        o_ref[qi * BS:(qi + 1) * BS, :] = (acc / l).astype(o_ref.dtype)


def _moba(proj, q_gain, k_gain, t0, t1, far, B, S):
    H = A_HEADS
    nb = S // MOBA_BLOCK
    dh = A_HEAD_DIM
    return pl.pallas_call(
        functools.partial(_moba_kernel, nb=nb),
        grid=(B, H),
        in_specs=[
            pl.BlockSpec((S, dh), lambda b, h: (b, h)),
            pl.BlockSpec((S, dh), lambda b, h: (b, H + h)),
            pl.BlockSpec((S, dh), lambda b, h: (b, 2 * H + h)),
            pl.BlockSpec((1, dh), lambda b, h: (0, 0)),
            pl.BlockSpec((1, dh), lambda b, h: (0, 0)),
            pl.BlockSpec((None, MOBA_BLOCK, MOBA_BLOCK), lambda b, h: (h, 0, 0)),
            pl.BlockSpec((None, MOBA_BLOCK, MOBA_BLOCK), lambda b, h: (h, 0, 0)),
            pl.BlockSpec((None, 8, MOBA_BLOCK), lambda b, h: (h, 0, 0)),
        ],
        out_specs=pl.BlockSpec((S, dh), lambda b, h: (b, h)),
        out_shape=jax.ShapeDtypeStruct((B * S, H * dh), BF16),
        compiler_params=_cparams(("parallel", "parallel")),
        name="moba",
    )(proj, proj, proj, q_gain, k_gain, t0, t1, far)


def _t5_bucket(rel):
    n = jnp.maximum(-rel, 0)
    max_exact = REL_BUCKETS // 2
    nf = jnp.maximum(n, 1).astype(F32)
    large = max_exact + (jnp.log(nf / max_exact) / math.log(REL_MAX_DIST / max_exact)
                         * (REL_BUCKETS - max_exact)).astype(jnp.int32)
    large = jnp.minimum(large, REL_BUCKETS - 1)
    return jnp.where(n < max_exact, n, large)


def _bias_tiles(rel_bias):
    assert MOBA_BLOCK + 1 >= REL_MAX_DIST
    offs = jnp.arange(MOBA_BLOCK)
    rel0 = offs[None, :] - offs[:, None]
    bias_hb = rel_bias.T.astype(F32)

    def tile(rel):
        onehot = jax.nn.one_hot(_t5_bucket(rel), REL_BUCKETS, dtype=F32)
        return jnp.einsum('hb,qkb->hqk', bias_hb, onehot, precision=HIGHEST)

    t0 = tile(rel0)
    t1 = tile(rel0 - MOBA_BLOCK)
    far = jnp.broadcast_to(bias_hb[:, REL_BUCKETS - 1][:, None, None], (A_HEADS, 8, MOBA_BLOCK))
    return t0, t1, far


def _gdn_kernel(q_ref, k_ref, v_ref, z_ref, sm_ref, cwq_ref, cwk_ref, cwv_ref, alog_ref, dtb_ref,
                ng_ref, o_ref, qs, ks, vs, bs, gs, *, S):
    C = DELTA_CHUNK
    G = GDN_GROUP
    dk = B_KEY_DIM
    HP = GDN_HEADS_PER_STEP
    NG = S // G
    sel_r = lax.broadcasted_iota(jnp.int32, (LANES, LANES), 0)

    def conv_silu(x_ref, w_ref, r0, first):
        w = w_ref[...]
        cur = x_ref[pl.ds(r0, G), :]
        above = x_ref[pl.ds(pl.multiple_of(jnp.maximum(r0 - 8, 0), 8), 8), :]
        xcat = jnp.concatenate([jnp.where(first, 0.0, above), cur], axis=0)
        y = cur * w[CONV_WIDTH - 1:CONV_WIDTH, :]
        for d in range(1, CONV_WIDTH):
            y = y + pltpu.roll(xcat, d, 0)[8:] * w[CONV_WIDTH - 1 - d:CONV_WIDTH - d, :]
        return _silu(y)

    def l2n(x):
        parts = [x[:, i * dk:(i + 1) * dk] for i in range(HP)]
        return jnp.concatenate(
            [p * lax.rsqrt(jnp.sum(p * p, axis=-1, keepdims=True) + EPS) for p in parts], axis=1)

    def prepare(gi, slot):
        r0 = pl.multiple_of(gi * G, G)
        first = gi == 0
        qs[slot] = l2n(conv_silu(q_ref, cwq_ref, r0, first)) * (dk ** -0.5)
        yield
        ks[slot] = l2n(conv_silu(k_ref, cwk_ref, r0, first))
        yield
        vs[slot] = conv_silu(v_ref, cwv_ref, r0, first)
        yield
        sm = sm_ref[pl.ds(r0, G), :]
        beta_all = _sigmoid(sm)
        z = sm + dtb_ref[...]
        softplus = jnp.maximum(z, 0.0) + jnp.log(1.0 + jnp.exp(-jnp.abs(z)))
        g_all = -jnp.exp(alog_ref[...]) * softplus
        for i in range(HP):
            hh = pl.program_id(1) * HP + i
            bs[slot, i] = _dot_exact01_rhs(beta_all, sel_r == hh)
            gs[slot, i] = _dot_exact01_rhs(g_all, sel_r == hh + B_HEADS)
            yield

    for _ in prepare(jnp.int32(0), 0):
        pass

    ri = lax.broadcasted_iota(jnp.int32, (G, G), 0)
    ci = lax.broadcasted_iota(jnp.int32, (G, G), 1)
    same = (ri // C) == (ci // C)
    strict = same & (ri > ci)
    causal = same & (ri >= ci)
    tri = causal
    strict_f = strict.astype(F32)
    eye = (ri == ci).astype(F32)
    ng = ng_ref[...]

    def group(gi, states):
        r0 = pl.multiple_of(gi * G, G)
        slot = gi % 2
        gens = [head_group(r0, slot, i, states[i]) for i in range(HP)]
        gens.append(prepare(jnp.minimum(gi + 1, NG - 1), 1 - slot))
        done = {}
        while len(done) < len(gens):
            for i in range(len(gens)):
                if i not in done:
                    try:
                        next(gens[i])
                    except StopIteration as stop:
                        done[i] = stop.value
        return tuple(done[i] for i in range(HP))

    def head_group(r0, slot, i, state):
        lanes = slice(i * dk, (i + 1) * dk)
        qg = qs[slot, :, lanes]
        kg = ks[slot, :, lanes]
        vg = vs[slot, :, lanes]
        bg = bs[slot, i]
        gg = gs[slot, i]
        gmat = jnp.concatenate([gg, gg], axis=1)
        dmat = _dot_exact01(tri, gmat * strict_f)
        gcum = _dot_exact01(tri, gg)
        yield
        decay = jnp.where(causal, jnp.exp(jnp.where(causal, dmat, 0.0)), 0.0)
        kb = kg * bg
        vb = vg * bg
        kk = _dot_nt(kb, kg)
        yield
        a = -(kk * decay) * strict_f
        t = eye + a
        p2 = _split2(a)
        for _ in range(int(math.log2(C)) - 1):
            p2 = _split2(_dot3(p2, p2))
            yield
            t = t + _dot3(_split2(t), p2)
            yield
        eg = jnp.exp(gcum)
        tb = t.astype(BF16)
        u = _mm(tb, vb.astype(BF16))
        w = _mm(tb, (kb * eg).astype(BF16))
        attn = _dot_nt(qg, kg) * decay
        yield
        qd = qg * eg
        outs = []
        vnews = []
        for c in range(G // C):
            sl = slice(c * C, (c + 1) * C)
            glast = gcum[(c + 1) * C - 1:(c + 1) * C, :]
            kd = kg[sl] * jnp.exp(glast - gcum[sl])
            sb = state.astype(BF16)
            v_new = u[sl] - _mm(w[sl].astype(BF16), sb)
            outs.append(_mm(qd[sl].astype(BF16), sb))
            vnews.append(v_new)
            yield
            state = state * jnp.exp(glast)[:, 0:1] + _dot_tn(kd, v_new)
            yield
        o = jnp.concatenate(outs, axis=0) + _dot(attn, jnp.concatenate(vnews, axis=0))
        on = o * lax.rsqrt(jnp.mean(o * o, axis=-1, keepdims=True) + EPS) * ng
        o_ref[pl.ds(r0, G), lanes] = (on * _silu(z_ref[pl.ds(r0, G), lanes])).astype(o_ref.dtype)
        return state

    lax.fori_loop(0, S // G, group, tuple(jnp.zeros((dk, B_VAL_DIM), F32) for _ in range(HP)))


def _gdn(proj, small, conv_w, alog_pad, dtb_pad, norm_gain, B, S):
    HP = GDN_HEADS_PER_STEP
    NH = B_HEADS // HP
    W = HP * B_KEY_DIM
    qoff = 3 * A_HEADS // HP
    blk = lambda off: pl.BlockSpec((S, W), lambda b, h: (b, off + h))
    cw = lambda off: pl.BlockSpec((CONV_WIDTH, W), lambda b, h: (0, off + h))
    one = pl.BlockSpec((1, LANES), lambda b, h: (0, 0))
    return pl.pallas_call(
        functools.partial(_gdn_kernel, S=S),
        grid=(B, NH),
        in_specs=[blk(qoff), blk(qoff + NH), blk(qoff + 2 * NH), blk(qoff + 3 * NH),
                  pl.BlockSpec((S, LANES), lambda b, h: (b, 0)),
                  cw(0), cw(NH), cw(2 * NH),
                  one, one, one],
        out_specs=pl.BlockSpec((S, W), lambda b, h: (b, h)),
        out_shape=jax.ShapeDtypeStruct((B * S, B_HEADS * B_VAL_DIM), BF16),
        scratch_shapes=([pltpu.VMEM((2, GDN_GROUP, W), F32)] * 3
                        + [pltpu.VMEM((2, HP, GDN_GROUP, LANES), F32)] * 2),
        compiler_params=_cparams(("parallel", "parallel")),
        name="gdn",
    )(proj, proj, proj, proj, small, conv_w, conv_w, conv_w, alog_pad, dtb_pad, norm_gain)


def _merge_kernel(oa_ref, ob_ref, wa_ref, wb_ref, ga_ref, gb_ref, o_ref):
    a = jnp.dot(oa_ref[...], wa_ref[...], preferred_element_type=F32)
    b = jnp.dot(ob_ref[...], wb_ref[...], preferred_element_type=F32)
    o_ref[...] = (_sigmoid(ga_ref[...]) * a + _sigmoid(gb_ref[...]) * b).astype(o_ref.dtype)


def _merge(o_a, o_b, w_up_a, w_up_b, proj, tm=512, tn=1024):
    T, K = o_a.shape
    D = w_up_a.shape[1]
    ga_off = (3 * A_HEADS + 4 * B_HEADS) * LANES // tn
    gb_off = ga_off + D // tn
    return pl.pallas_call(
        _merge_kernel,
        grid=(T // tm, D // tn),
        in_specs=[
            pl.BlockSpec((tm, K), lambda i, j: (i, 0)),
            pl.BlockSpec((tm, K), lambda i, j: (i, 0)),
            pl.BlockSpec((K, tn), lambda i, j: (0, j)),
            pl.BlockSpec((K, tn), lambda i, j: (0, j)),
            pl.BlockSpec((tm, tn), lambda i, j: (i, ga_off + j)),
            pl.BlockSpec((tm, tn), lambda i, j: (i, gb_off + j)),
        ],
        out_specs=pl.BlockSpec((tm, tn), lambda i, j: (i, j)),
        out_shape=jax.ShapeDtypeStruct((T, D), BF16),
        compiler_params=_cparams(("parallel", "parallel")),
        name="merge",
    )(o_a, o_b, w_up_a, w_up_b, proj, proj)


def _out_proj_kernel(m_ref, w_ref, x_ref, g_ref, y_ref, h_ref):
    y = x_ref[...] + jnp.dot(m_ref[...], w_ref[...], preferred_element_type=F32)
    y_ref[...] = y
    ms = jnp.mean(y * y, axis=-1, keepdims=True)
    h_ref[...] = (y * lax.rsqrt(ms + EPS) * g_ref[...]).astype(h_ref.dtype)


def _out_proj(merged, w_out, x2, gain2, tm=512):
    T, D = x2.shape
    return pl.pallas_call(
        _out_proj_kernel,
        grid=(T // tm,),
        in_specs=[
            pl.BlockSpec((tm, D), lambda i: (i, 0)),
            pl.BlockSpec((D, D), lambda i: (0, 0)),
            pl.BlockSpec((tm, D), lambda i: (i, 0)),
            pl.BlockSpec((1, D), lambda i: (0, 0)),
        ],
        out_specs=[pl.BlockSpec((tm, D), lambda i: (i, 0)), pl.BlockSpec((tm, D), lambda i: (i, 0))],
        out_shape=[jax.ShapeDtypeStruct((T, D), F32), jax.ShapeDtypeStruct((T, D), BF16)],
        compiler_params=_cparams(("parallel",)),
        name="out_proj",
    )(merged, w_out, x2, gain2)


def _matmul_kernel(a_ref, b_ref, o_ref):
    o_ref[...] = jnp.dot(a_ref[...], b_ref[...], preferred_element_type=F32)


def _peer_query(h2, w_query, tm=512):
    T, D = h2.shape
    N = w_query.shape[1]
    return pl.pallas_call(
        _matmul_kernel,
        grid=(T // tm,),
        in_specs=[pl.BlockSpec((tm, D), lambda i: (i, 0)), pl.BlockSpec((D, N), lambda i: (0, 0))],
        out_specs=pl.BlockSpec((tm, N), lambda i: (i, 0)),
        out_shape=jax.ShapeDtypeStruct((T, N), F32),
        compiler_params=_cparams(("parallel",)),
        name="peer_query",
    )(h2, w_query)


_NO_POS = 2 ** 30


def _topk_rows(s, k, pos):
    vals, idxs = [], []
    for _ in range(k):
        m = jnp.max(s, axis=0, keepdims=True)
        i = jnp.min(jnp.where(s == m, pos, _NO_POS), axis=0, keepdims=True)
        vals.append(m)
        idxs.append(i)
        s = jnp.where(pos == i, -jnp.inf, s)
    return jnp.concatenate(vals, axis=0), jnp.concatenate(idxs, axis=0)


def _peer_route_kernel(q_ref, keys_ref, idx_ref, gate_ref, *, tt):
    K = PEER_TOPK
    half = PEER_QDIM // 2
    pos_k = lax.broadcasted_iota(jnp.int32, (PEER_NKEYS, tt), 0)
    row_k = lax.broadcasted_iota(jnp.int32, (K, tt), 0)
    row_8 = lax.broadcasted_iota(jnp.int32, (8, tt), 0)
    assert K == 16 and all((i < 4) or (i < 8 and j < 4) or (j == 0)
                           for i in range(K) for j in range(K) if (i + 1) * (j + 1) <= K)
    idx_rows, gate_rows = [], []
    for h in range(PEER_HEADS):
        tops = []
        for p in range(2):
            qh = q_ref[:, (2 * h + p) * half:(2 * h + p + 1) * half]
            sc = _dot_nt_hi(keys_ref[h, p], qh)
            tops.append(_topk_rows(sc, K, pos_k))
        (s0, i0), (s1, i1) = tops
        cb, pb, ib = [], [], []
        for i in range(4):
            cb.append(s0[i:i + 1] + s1)
            pb.append(i * K + row_k)
            ib.append(i0[i:i + 1] * PEER_NKEYS + i1)
        for j in range(4):
            cb.append(jnp.where(row_8 >= 4, s0[0:8] + s1[j:j + 1], -jnp.inf))
            pb.append(row_8 * K + j)
            ib.append(i0[0:8] * PEER_NKEYS + i1[j:j + 1])
        cb.append(s0[8:16] + s1[0:1])
        pb.append((row_8 + 8) * K)
        ib.append(i0[8:16] * PEER_NKEYS + i1[0:1])
        cand = jnp.concatenate(cb, axis=0)
        pos_c = jnp.concatenate(pb, axis=0)
        cidx = jnp.concatenate(ib, axis=0)
        best, bpos = _topk_rows(cand, K, pos_c)
        eidx = jnp.concatenate(
            [jnp.min(jnp.where(pos_c == bpos[r:r + 1], cidx, PEER_NKEYS * PEER_NKEYS),
                     axis=0, keepdims=True) for r in range(K)], axis=0)
        e = jnp.exp(best - best[0:1])
        g = e / jnp.sum(e, axis=0, keepdims=True)
        zero = jnp.zeros_like(g[0:1])
        for r in range(K):
            gate_rows += [zero, g[r:r + 1]]
        idx_rows.append(eidx)
    idx_ref[...] = jnp.concatenate(idx_rows, axis=0).T
    gate_ref[...] = jnp.concatenate(gate_rows, axis=0).T


def _peer_route(qry, sub_keys, tt=256):
    T, N = qry.shape
    PK = PEER_HEADS * PEER_TOPK
    return pl.pallas_call(
        functools.partial(_peer_route_kernel, tt=tt),
        grid=(T // tt,),
        in_specs=[pl.BlockSpec((tt, N), lambda i: (i, 0)),
                  pl.BlockSpec(sub_keys.shape, lambda i: (0, 0, 0, 0))],
        out_specs=[pl.BlockSpec((tt, PK), lambda i: (i, 0)), pl.BlockSpec((tt, 2 * PK), lambda i: (i, 0))],
        out_shape=[jax.ShapeDtypeStruct((T, PK), jnp.int32), jax.ShapeDtypeStruct((T, 2 * PK), F32)],
        compiler_params=_cparams(("parallel",)),
        name="peer_route",
    )(qry, sub_keys)


PEER_TOKENS_PER_STEP = 4
PEER_AHEAD = 2
PEER_TOKEN_GROUP = 16
PEER_BANKS = PEER_TOKEN_GROUP // PEER_TOKENS_PER_STEP
PEER_SLOTS = PEER_BANKS * PEER_TOKENS_PER_STEP


def _gelu(x):
    return 0.5 * x * (1.0 + lax.erf(x * (2.0 ** -0.5)))


def _pack_expert_table(u, v):
    def pack(x):
        E, D = x.shape
        b = lax.bitcast_convert_type(x.astype(BF16), jnp.uint16).astype(jnp.uint32)
        return (b[:, :D // 2] | (b[:, D // 2:] << 16)).reshape(E, D // (2 * LANES), LANES)
    return jnp.concatenate([pack(u), pack(v)], axis=1)


def _peer_expert_kernel(idx_ref, gate_ref, h_ref, y_ref, tab_ref, o_ref, buf, sem, *, tt, D):
    PK = PEER_HEADS * PEER_TOPK
    GT = PEER_TOKEN_GROUP
    HD = D // 2
    R = HD // LANES

    def issue(t, slot, k0=0, k1=PK):
        for k in range(k0, k1):
            e = idx_ref[t, k]
            pltpu.make_async_copy(tab_ref.at[e], buf.at[slot, :, k], sem.at[slot]).start(priority=k % 2)

    def wait(slot):
        pltpu.make_async_copy(buf.at[slot], buf.at[slot], sem.at[slot]).wait()

    TPS = PEER_TOKENS_PER_STEP

    AHEAD = PEER_AHEAD
    NSTEP = GT // TPS
    assert AHEAD < PEER_BANKS

    for t in range(AHEAD * TPS):
        issue(t, t)

    even = lax.broadcasted_iota(jnp.int32, (1, 2 * PK), 1) % 2 == 0

    def tokens(j0, bank, hr, gg, prefetch):
        zs = [jnp.zeros((1, 2 * PK), F32) for _ in range(TPS)]
        for r in range(R):
            for i in range(TPS):
                j = j0 + i
                ub = pltpu.bitcast(buf[bank + i, r], BF16)
                res = lax.dot_general(hr[r], ub, (((1,), (1,)), ((), ())), preferred_element_type=F32)
                zs[i] = zs[i] + jnp.where(even, res[j:j + 1], res[GT + j:GT + j + 1])
                prefetch(i, r)
        a2s = []
        for i in range(TPS):
            j = j0 + i
            s = zs[i] + pltpu.roll(zs[i], 1, 1)
            a = _gelu(s) * gg[j:j + 1]
            a2s.append(jnp.concatenate([pltpu.roll(a, 2 * PK - 1, 1), a], axis=0).astype(BF16))
        o2 = [[] for _ in range(TPS)]
        for r in range(R):
            for i in range(TPS):
                o2[i].append(jnp.dot(a2s[i], pltpu.bitcast(buf[bank + i, R + r], BF16),
                                     preferred_element_type=F32))
                prefetch(i, R + r)
        return [jnp.concatenate([o[0:1] for o in o2[i]] + [o[1:2] for o in o2[i]], axis=1)
                for i in range(TPS)]

    def group(t0, last):
        hg = h_ref[pl.ds(t0, GT), :]
        gg = gate_ref[pl.ds(t0, GT), :]
        hr = [jnp.concatenate([hg[:, r * LANES:(r + 1) * LANES],
                               hg[:, HD + r * LANES:HD + (r + 1) * LANES]], axis=0) for r in range(R)]
        outs = []
        for step in range(NSTEP):
            bank, nbank = step * TPS, ((step + AHEAD) % NSTEP) * TPS
            for i in range(TPS):
                wait(bank + i)

            def prefetch(i, stage, step=step, nbank=nbank):
                if not (last and step + AHEAD >= NSTEP):
                    per = PK // (2 * R)
                    issue(t0 + (step + AHEAD) * TPS + i, nbank + i, stage * per, (stage + 1) * per)
            outs += tokens(step * TPS, bank, hr, gg, prefetch)
        o_ref[pl.ds(t0, GT), :] = y_ref[pl.ds(t0, GT), :] + jnp.concatenate(outs, axis=0)

    def body(gi, c):
        group(pl.multiple_of(gi * GT, GT), False)
        return c

    lax.fori_loop(0, tt // GT - 1, body, 0)
    group(tt - GT, True)


def _peer_experts(idx, gate2, h2, y1, table, tt=256):
    T, D = y1.shape
    PK = PEER_HEADS * PEER_TOPK
    return pl.pallas_call(
        functools.partial(_peer_expert_kernel, tt=tt, D=D),
        grid=(T // tt,),
        in_specs=[
            pl.BlockSpec((tt, PK), lambda i: (i, 0), memory_space=pltpu.SMEM),
            pl.BlockSpec((tt, 2 * PK), lambda i: (i, 0)),
            pl.BlockSpec((tt, D), lambda i: (i, 0)),
            pl.BlockSpec((tt, D), lambda i: (i, 0)),
            pl.BlockSpec(memory_space=pl.ANY),
        ],
        out_specs=pl.BlockSpec((tt, D), lambda i: (i, 0)),
        out_shape=jax.ShapeDtypeStruct((T, D), F32),
        scratch_shapes=[pltpu.VMEM((PEER_SLOTS, table.shape[1], PK, LANES), jnp.uint32),
                        pltpu.SemaphoreType.DMA((PEER_SLOTS,))],
        compiler_params=pltpu.CompilerParams(dimension_semantics=("arbitrary",),
                                             vmem_limit_bytes=VMEM_LIMIT, disable_bounds_checks=True),
        name="peer_experts",
    )(idx, gate2, h2, y1, table)


def _layer(x, rel_bias, norm1_gain, w_in, conv_w, a_log, dt_bias, q_norm_gain, k_norm_gain,
           gdn_norm_gain, w_up_a, w_up_b, w_out, norm2_gain, peer_w_query, peer_sub_keys,
           peer_u, peer_v):
    B, S, D = x.shape
    T = B * S
    x2 = x.reshape(T, D)
    n_main = 3 * A_HEADS * A_HEAD_DIM + 4 * B_HEADS * B_KEY_DIM
    n_small = 2 * B_HEADS
    w_main = jnp.concatenate([w_in[:, :n_main], w_in[:, n_main + n_small:]], axis=1).astype(BF16)
    w_small = jnp.pad(w_in[:, n_main:n_main + n_small], ((0, 0), (0, LANES - n_small))).astype(BF16)
    proj, small = _in_proj(x2, norm1_gain[None, :], w_main, w_small)

    t0, t1, far = _bias_tiles(rel_bias)
    o_a = _moba(proj, q_norm_gain[None, :], k_norm_gain[None, :], t0, t1, far, B, S)

    pad8 = lambda v: jnp.pad(v, (B_HEADS, LANES - 2 * B_HEADS))[None, :]
    o_b = _gdn(proj, small, conv_w, pad8(a_log), pad8(dt_bias), gdn_norm_gain[None, :], B, S)

    merged = _merge(o_a, o_b, w_up_a.astype(BF16), w_up_b.astype(BF16), proj)
    y1, h2 = _out_proj(merged, w_out.astype(BF16), x2, norm2_gain[None, :])

    qry = _peer_query(h2, peer_w_query.astype(BF16))
    idx, gate2 = _peer_route(qry, peer_sub_keys)
    y = _peer_experts(idx, gate2, h2, y1, _pack_expert_table(peer_u, peer_v))
    return y.reshape(B, S, D)


def kernel(x, rel_bias, norm1_gain, w_in, conv_w, a_log, dt_bias, q_norm_gain, k_norm_gain,
           gdn_norm_gain, w_up_a, w_up_b, w_out, norm2_gain, peer_w_query, peer_sub_keys,
           peer_u, peer_v):
    depth = w_in.shape[0]
    for l in range(depth):
        x = _layer(x, rel_bias, norm1_gain[l], w_in[l], conv_w[l], a_log[l], dt_bias[l],
                   q_norm_gain[l], k_norm_gain[l], gdn_norm_gain[l], w_up_a[l], w_up_b[l],
                   w_out[l], norm2_gain[l], peer_w_query[l], peer_sub_keys[l], peer_u[l], peer_v[l])
    return x
```

```python
import functools
import math

import jax
import jax.numpy as jnp
import numpy as np
from jax import lax
from jax.experimental import pallas as pl
from jax.experimental.pallas import tpu as pltpu

F32 = jnp.float32
BF16 = jnp.bfloat16
HIGHEST = lax.Precision.HIGHEST

LANES = 128
EPS = 1e-6

A_HEADS = 8
A_HEAD_DIM = 128
MOBA_BLOCK = 256
MOBA_TOPK = 3
REL_BUCKETS = 32
REL_MAX_DIST = 128
B_HEADS = 8
B_KEY_DIM = 128
B_VAL_DIM = 128
CONV_WIDTH = 4
DELTA_CHUNK = 64
GDN_GROUP = 256
GDN_HEADS_PER_STEP = 4
PEER_HEADS = 8
PEER_NKEYS = 128
PEER_QDIM = 256
PEER_TOPK = 16

VMEM_LIMIT = 56 * 1024 * 1024


def _cparams(sem):
    return pltpu.CompilerParams(dimension_semantics=sem, vmem_limit_bytes=VMEM_LIMIT)


def _dot(a, b):
    return jnp.dot(a.astype(BF16), b.astype(BF16), preferred_element_type=F32)


def _dot_nt(a, b):
    return lax.dot_general(a.astype(BF16), b.astype(BF16), (((1,), (1,)), ((), ())),
                           preferred_element_type=F32)


def _dot_tn(a, b):
    return lax.dot_general(a.astype(BF16), b.astype(BF16), (((0,), (0,)), ((), ())),
                           preferred_element_type=F32)


def _dot_nt_hi(a, b):
    return lax.dot_general(a, b, (((1,), (1,)), ((), ())), preferred_element_type=F32,
                           precision=HIGHEST)


def _split2(a):
    hi = a.astype(BF16)
    return hi, (a - hi.astype(F32)).astype(BF16)


def _mm(a, b):
    return jnp.dot(a, b, preferred_element_type=F32)


def _dot3(a2, b2):
    (ah, al), (bh, bl) = a2, b2
    return _mm(ah, bh) + (_mm(ah, bl) + _mm(al, bh))


def _dot_exact01(m01, x):
    m = m01.astype(BF16)
    hi = x.astype(BF16)
    r1 = x - hi.astype(F32)
    mid = r1.astype(BF16)
    lo = (r1 - mid.astype(F32)).astype(BF16)
    return _mm(m, hi) + (_mm(m, mid) + _mm(m, lo))


def _dot_exact01_rhs(x, m01):
    m = m01.astype(BF16)
    hi = x.astype(BF16)
    r1 = x - hi.astype(F32)
    mid = r1.astype(BF16)
    lo = (r1 - mid.astype(F32)).astype(BF16)
    return _mm(hi, m) + (_mm(mid, m) + _mm(lo, m))


def _sigmoid(x):
    return 1.0 / (1.0 + jnp.exp(-x))


def _silu(x):
    return x * _sigmoid(x)


def _in_proj_kernel(x_ref, g_ref, w_ref, ws_ref, o_ref, os_ref, h_scr):
    @pl.when(pl.program_id(1) == 0)
    def _():
        x = x_ref[...]
        ms = jnp.mean(x * x, axis=-1, keepdims=True)
        h = (x * lax.rsqrt(ms + EPS) * g_ref[...]).astype(BF16)
        h_scr[...] = h
        os_ref[...] = jnp.dot(h, ws_ref[...], preferred_element_type=F32)

    o_ref[...] = jnp.dot(h_scr[...], w_ref[...], preferred_element_type=F32)


def _in_proj(x2, gain, w_main, w_small, tm=1024, tn=1024):
    T, D = x2.shape
    N = w_main.shape[1]
    return pl.pallas_call(
        _in_proj_kernel,
        grid=(T // tm, N // tn),
        in_specs=[
            pl.BlockSpec((tm, D), lambda i, j: (i, 0)),
            pl.BlockSpec((1, D), lambda i, j: (0, 0)),
            pl.BlockSpec((D, tn), lambda i, j: (0, j)),
            pl.BlockSpec((D, LANES), lambda i, j: (0, 0)),
        ],
        out_specs=[
            pl.BlockSpec((tm, tn), lambda i, j: (i, j)),
            pl.BlockSpec((tm, LANES), lambda i, j: (i, 0)),
        ],
        out_shape=[jax.ShapeDtypeStruct((T, N), F32), jax.ShapeDtypeStruct((T, LANES), F32)],
        scratch_shapes=[pltpu.VMEM((tm, D), BF16)],
        compiler_params=_cparams(("parallel", "arbitrary")),
        name="in_proj",
    )(x2, gain, w_main, w_small)


def _moba_kernel(q_ref, k_ref, v_ref, qg_ref, kg_ref, t0_ref, t1_ref, far_ref, o_ref, *, nb):
    BS = MOBA_BLOCK
    scale = A_HEAD_DIM ** -0.5
    q = q_ref[...]
    k = k_ref[...]
    qn = q * lax.rsqrt(jnp.mean(q * q, axis=-1, keepdims=True) + EPS) * qg_ref[...]
    kn = k * lax.rsqrt(jnp.mean(k * k, axis=-1, keepdims=True) + EPS) * kg_ref[...]
    kmean = jnp.concatenate(
        [jnp.mean(kn[j * BS:(j + 1) * BS], axis=0, keepdims=True) for j in range(nb)], axis=0)
    knb = kn.astype(BF16)
    vb = v_ref[...].astype(BF16)
    t0 = t0_ref[...]
    t1 = t1_ref[...]
    far = far_ref[0:1, :]
    rows = lax.broadcasted_iota(jnp.int32, (BS, BS), 0)
    cols = lax.broadcasted_iota(jnp.int32, (BS, BS), 1)
    causal = cols <= rows
    blk_iota = lax.broadcasted_iota(jnp.int32, (nb, BS), 0)
    neg = -jnp.inf
    for qi in range(nb):
        qblk = qn[qi * BS:(qi + 1) * BS]
        if qi > 0:
            gate = _dot_nt_hi(kmean, qblk)
            gate = jnp.where(blk_iota < qi, gate, neg)
            sel = jnp.zeros((nb, BS), F32)
            for _ in range(min(MOBA_TOPK, nb)):
                m = jnp.max(gate, axis=0, keepdims=True)
                idx = jnp.min(jnp.where(gate == m, blk_iota, nb), axis=0, keepdims=True)
                sel = jnp.where((blk_iota == idx) & (m > neg), 1.0, sel)
                gate = jnp.where(blk_iota == idx, neg, gate)
            self32 = jnp.concatenate([sel, jnp.zeros((LANES - nb, BS), F32)], axis=0).T
        else:
            self32 = jnp.zeros((BS, LANES), F32)
        s_all = _dot_nt(qblk, knb[:(qi + 1) * BS]) * scale
        blocks = []
        for j in range(qi + 1):
            s = s_all[:, j * BS:(j + 1) * BS]
            if j == qi:
                s = jnp.where(causal, s + t0, neg)
            else:
                bias = t1 if j == qi - 1 else far
                picked = self32[:, j:j + 1] > 0.5
                s = jnp.where(picked, s + bias, neg)
            blocks.append(s)
        smax = blocks[-1]
        for s in blocks[:-1]:
            smax = jnp.maximum(smax, s)
        m = jnp.max(smax, axis=1, keepdims=True)
        psum = jnp.zeros((BS, BS), F32)
        acc = jnp.zeros((BS, A_HEAD_DIM), F32)
        for j, s in enumerate(blocks):
            p = jnp.exp(s - m)
            psum = psum + p
            acc = acc + _dot(p, vb[j * BS:(j + 1) * BS])
        l = jnp.sum(psum, axis=1, keepdims=True)
        o_ref[qi * BS:(qi + 1) * BS, :] = (acc / l).astype(o_ref.dtype)


def _moba(proj, q_gain, k_gain, t0, t1, far, B, S):
    H = A_HEADS
    nb = S // MOBA_BLOCK
    dh = A_HEAD_DIM
    return pl.pallas_call(
        functools.partial(_moba_kernel, nb=nb),
        grid=(B, H),
        in_specs=[
            pl.BlockSpec((S, dh), lambda b, h: (b, h)),
            pl.BlockSpec((S, dh), lambda b, h: (b, H + h)),
            pl.BlockSpec((S, dh), lambda b, h: (b, 2 * H + h)),
            pl.BlockSpec((1, dh), lambda b, h: (0, 0)),
            pl.BlockSpec((1, dh), lambda b, h: (0, 0)),
            pl.BlockSpec((None, MOBA_BLOCK, MOBA_BLOCK), lambda b, h: (h, 0, 0)),
            pl.BlockSpec((None, MOBA_BLOCK, MOBA_BLOCK), lambda b, h: (h, 0, 0)),
            pl.BlockSpec((None, 8, MOBA_BLOCK), lambda b, h: (h, 0, 0)),
        ],
        out_specs=pl.BlockSpec((S, dh), lambda b, h: (b, h)),
        out_shape=jax.ShapeDtypeStruct((B * S, H * dh), BF16),
        compiler_params=_cparams(("parallel", "parallel")),
        name="moba",
    )(proj, proj, proj, q_gain, k_gain, t0, t1, far)


def _t5_bucket(rel):
    n = jnp.maximum(-rel, 0)
    max_exact = REL_BUCKETS // 2
    nf = jnp.maximum(n, 1).astype(F32)
    large = max_exact + (jnp.log(nf / max_exact) / math.log(REL_MAX_DIST / max_exact)
                         * (REL_BUCKETS - max_exact)).astype(jnp.int32)
    large = jnp.minimum(large, REL_BUCKETS - 1)
    return jnp.where(n < max_exact, n, large)


def _bias_tiles(rel_bias):
    assert MOBA_BLOCK + 1 >= REL_MAX_DIST
    offs = jnp.arange(MOBA_BLOCK)
    rel0 = offs[None, :] - offs[:, None]
    bias_hb = rel_bias.T.astype(F32)

    def tile(rel):
        onehot = jax.nn.one_hot(_t5_bucket(rel), REL_BUCKETS, dtype=F32)
        return jnp.einsum('hb,qkb->hqk', bias_hb, onehot, precision=HIGHEST)

    t0 = tile(rel0)
    t1 = tile(rel0 - MOBA_BLOCK)
    far = jnp.broadcast_to(bias_hb[:, REL_BUCKETS - 1][:, None, None], (A_HEADS, 8, MOBA_BLOCK))
    return t0, t1, far


def _gdn_kernel(q_ref, k_ref, v_ref, z_ref, sm_ref, cwq_ref, cwk_ref, cwv_ref, alog_ref, dtb_ref,
                ng_ref, o_ref, qs, ks, vs, bs, gs, *, S):
    C = DELTA_CHUNK
    G = GDN_GROUP
    dk = B_KEY_DIM
    HP = GDN_HEADS_PER_STEP
    NG = S // G
    sel_r = lax.broadcasted_iota(jnp.int32, (LANES, LANES), 0)

    def conv_silu(x_ref, w_ref, r0, first):
        w = w_ref[...]
        cur = x_ref[pl.ds(r0, G), :]
        above = x_ref[pl.ds(pl.multiple_of(jnp.maximum(r0 - 8, 0), 8), 8), :]
        xcat = jnp.concatenate([jnp.where(first, 0.0, above), cur], axis=0)
        y = cur * w[CONV_WIDTH - 1:CONV_WIDTH, :]
        for d in range(1, CONV_WIDTH):
            y = y + pltpu.roll(xcat, d, 0)[8:] * w[CONV_WIDTH - 1 - d:CONV_WIDTH - d, :]
        return _silu(y)

    def l2n(x):
        parts = [x[:, i * dk:(i + 1) * dk] for i in range(HP)]
        return jnp.concatenate(
            [p * lax.rsqrt(jnp.sum(p * p, axis=-1, keepdims=True) + EPS) for p in parts], axis=1)

    def prepare(gi, slot):
        r0 = pl.multiple_of(gi * G, G)
        first = gi == 0
        qs[slot] = l2n(conv_silu(q_ref, cwq_ref, r0, first)) * (dk ** -0.5)
        yield
        ks[slot] = l2n(conv_silu(k_ref, cwk_ref, r0, first))
        yield
        vs[slot] = conv_silu(v_ref, cwv_ref, r0, first)
        yield
        sm = sm_ref[pl.ds(r0, G), :]
        beta_all = _sigmoid(sm)
        z = sm + dtb_ref[...]
        softplus = jnp.maximum(z, 0.0) + jnp.log(1.0 + jnp.exp(-jnp.abs(z)))
        g_all = -jnp.exp(alog_ref[...]) * softplus
        for i in range(HP):
            hh = pl.program_id(1) * HP + i
            bs[slot, i] = _dot_exact01_rhs(beta_all, sel_r == hh)
            gs[slot, i] = _dot_exact01_rhs(g_all, sel_r == hh + B_HEADS)
            yield

    for _ in prepare(jnp.int32(0), 0):
        pass

    ri = lax.broadcasted_iota(jnp.int32, (G, G), 0)
    ci = lax.broadcasted_iota(jnp.int32, (G, G), 1)
    same = (ri // C) == (ci // C)
    strict = same & (ri > ci)
    causal = same & (ri >= ci)
    tri = causal
    strict_f = strict.astype(F32)
    eye = (ri == ci).astype(F32)
    ng = ng_ref[...]

    def group(gi, states):
        r0 = pl.multiple_of(gi * G, G)
        slot = gi % 2
        gens = [head_group(r0, slot, i, states[i]) for i in range(HP)]
        gens.append(prepare(jnp.minimum(gi + 1, NG - 1), 1 - slot))
        done = {}
        while len(done) < len(gens):
            for i in range(len(gens)):
                if i not in done:
                    try:
                        next(gens[i])
                    except StopIteration as stop:
                        done[i] = stop.value
        return tuple(done[i] for i in range(HP))

    def head_group(r0, slot, i, state):
        lanes = slice(i * dk, (i + 1) * dk)
        qg = qs[slot, :, lanes]
        kg = ks[slot, :, lanes]
        vg = vs[slot, :, lanes]
        bg = bs[slot, i]
        gg = gs[slot, i]
        gmat = jnp.concatenate([gg, gg], axis=1)
        dmat = _dot_exact01(tri, gmat * strict_f)
        gcum = _dot_exact01(tri, gg)
        yield
        decay = jnp.where(causal, jnp.exp(jnp.where(causal, dmat, 0.0)), 0.0)
        kb = kg * bg
        vb = vg * bg
        kk = _dot_nt(kb, kg)
        yield
        a = -(kk * decay) * strict_f
        t = eye + a
        p2 = _split2(a)
        for _ in range(int(math.log2(C)) - 1):
            p2 = _split2(_dot3(p2, p2))
            yield
            t = t + _dot3(_split2(t), p2)
            yield
        eg = jnp.exp(gcum)
        tb = t.astype(BF16)
        u = _mm(tb, vb.astype(BF16))
        w = _mm(tb, (kb * eg).astype(BF16))
        attn = _dot_nt(qg, kg) * decay
        yield
        qd = qg * eg
        outs = []
        vnews = []
        for c in range(G // C):
            sl = slice(c * C, (c + 1) * C)
            glast = gcum[(c + 1) * C - 1:(c + 1) * C, :]
            kd = kg[sl] * jnp.exp(glast - gcum[sl])
            sb = state.astype(BF16)
            v_new = u[sl] - _mm(w[sl].astype(BF16), sb)
            outs.append(_mm(qd[sl].astype(BF16), sb))
            vnews.append(v_new)
            yield
            state = state * jnp.exp(glast)[:, 0:1] + _dot_tn(kd, v_new)
            yield
        o = jnp.concatenate(outs, axis=0) + _dot(attn, jnp.concatenate(vnews, axis=0))
        on = o * lax.rsqrt(jnp.mean(o * o, axis=-1, keepdims=True) + EPS) * ng
        o_ref[pl.ds(r0, G), lanes] = (on * _silu(z_ref[pl.ds(r0, G), lanes])).astype(o_ref.dtype)
        return state

    lax.fori_loop(0, S // G, group, tuple(jnp.zeros((dk, B_VAL_DIM), F32) for _ in range(HP)))


def _gdn(proj, small, conv_w, alog_pad, dtb_pad, norm_gain, B, S):
    HP = GDN_HEADS_PER_STEP
    NH = B_HEADS // HP
    W = HP * B_KEY_DIM
    qoff = 3 * A_HEADS // HP
    blk = lambda off: pl.BlockSpec((S, W), lambda b, h: (b, off + h))
    cw = lambda off: pl.BlockSpec((CONV_WIDTH, W), lambda b, h: (0, off + h))
    one = pl.BlockSpec((1, LANES), lambda b, h: (0, 0))
    return pl.pallas_call(
        functools.partial(_gdn_kernel, S=S),
        grid=(B, NH),
        in_specs=[blk(qoff), blk(qoff + NH), blk(qoff + 2 * NH), blk(qoff + 3 * NH),
                  pl.BlockSpec((S, LANES), lambda b, h: (b, 0)),
                  cw(0), cw(NH), cw(2 * NH),
                  one, one, one],
        out_specs=pl.BlockSpec((S, W), lambda b, h: (b, h)),
        out_shape=jax.ShapeDtypeStruct((B * S, B_HEADS * B_VAL_DIM), BF16),
        scratch_shapes=([pltpu.VMEM((2, GDN_GROUP, W), F32)] * 3
                        + [pltpu.VMEM((2, HP, GDN_GROUP, LANES), F32)] * 2),
        compiler_params=_cparams(("parallel", "parallel")),
        name="gdn",
    )(proj, proj, proj, proj, small, conv_w, conv_w, conv_w, alog_pad, dtb_pad, norm_gain)


def _merge_kernel(oa_ref, ob_ref, wa_ref, wb_ref, ga_ref, gb_ref, o_ref):
    a = jnp.dot(oa_ref[...], wa_ref[...], preferred_element_type=F32)
    b = jnp.dot(ob_ref[...], wb_ref[...], preferred_element_type=F32)
    o_ref[...] = (_sigmoid(ga_ref[...]) * a + _sigmoid(gb_ref[...]) * b).astype(o_ref.dtype)


def _merge(o_a, o_b, w_up_a, w_up_b, proj, tm=512, tn=1024):
    T, K = o_a.shape
    D = w_up_a.shape[1]
    ga_off = (3 * A_HEADS + 4 * B_HEADS) * LANES // tn
    gb_off = ga_off + D // tn
    return pl.pallas_call(
        _merge_kernel,
        grid=(T // tm, D // tn),
        in_specs=[
            pl.BlockSpec((tm, K), lambda i, j: (i, 0)),
            pl.BlockSpec((tm, K), lambda i, j: (i, 0)),
            pl.BlockSpec((K, tn), lambda i, j: (0, j)),
            pl.BlockSpec((K, tn), lambda i, j: (0, j)),
            pl.BlockSpec((tm, tn), lambda i, j: (i, ga_off + j)),
            pl.BlockSpec((tm, tn), lambda i, j: (i, gb_off + j)),
        ],
        out_specs=pl.BlockSpec((tm, tn), lambda i, j: (i, j)),
        out_shape=jax.ShapeDtypeStruct((T, D), BF16),
        compiler_params=_cparams(("parallel", "parallel")),
        name="merge",
    )(o_a, o_b, w_up_a, w_up_b, proj, proj)


def _out_proj_kernel(m_ref, w_ref, x_ref, g_ref, y_ref, h_ref):
    y = x_ref[...] + jnp.dot(m_ref[...], w_ref[...], preferred_element_type=F32)
    y_ref[...] = y
    ms = jnp.mean(y * y, axis=-1, keepdims=True)
    h_ref[...] = (y * lax.rsqrt(ms + EPS) * g_ref[...]).astype(h_ref.dtype)


def _out_proj(merged, w_out, x2, gain2, tm=512):
    T, D = x2.shape
    return pl.pallas_call(
        _out_proj_kernel,
        grid=(T // tm,),
        in_specs=[
            pl.BlockSpec((tm, D), lambda i: (i, 0)),
            pl.BlockSpec((D, D), lambda i: (0, 0)),
            pl.BlockSpec((tm, D), lambda i: (i, 0)),
            pl.BlockSpec((1, D), lambda i: (0, 0)),
        ],
        out_specs=[pl.BlockSpec((tm, D), lambda i: (i, 0)), pl.BlockSpec((tm, D), lambda i: (i, 0))],
        out_shape=[jax.ShapeDtypeStruct((T, D), F32), jax.ShapeDtypeStruct((T, D), BF16)],
        compiler_params=_cparams(("parallel",)),
        name="out_proj",
    )(merged, w_out, x2, gain2)


def _matmul_kernel(a_ref, b_ref, o_ref):
    o_ref[...] = jnp.dot(a_ref[...], b_ref[...], preferred_element_type=F32)


def _peer_query(h2, w_query, tm=512):
    T, D = h2.shape
    N = w_query.shape[1]
    return pl.pallas_call(
        _matmul_kernel,
        grid=(T // tm,),
        in_specs=[pl.BlockSpec((tm, D), lambda i: (i, 0)), pl.BlockSpec((D, N), lambda i: (0, 0))],
        out_specs=pl.BlockSpec((tm, N), lambda i: (i, 0)),
        out_shape=jax.ShapeDtypeStruct((T, N), F32),
        compiler_params=_cparams(("parallel",)),
        name="peer_query",
    )(h2, w_query)


_NO_POS = 2 ** 30


def _topk_rows(s, k, pos):
    vals, idxs = [], []
    for _ in range(k):
        m = jnp.max(s, axis=0, keepdims=True)
        i = jnp.min(jnp.where(s == m, pos, _NO_POS), axis=0, keepdims=True)
        vals.append(m)
        idxs.append(i)
        s = jnp.where(pos == i, -jnp.inf, s)
    return jnp.concatenate(vals, axis=0), jnp.concatenate(idxs, axis=0)


def _sort_network(n):
    def merge(lo, hi, r):
        step = r * 2
        if step < hi - lo:
            yield from merge(lo, hi, step)
            yield from merge(lo + r, hi, step)
            yield from ((i, i + r) for i in range(lo + r, hi - r, step))
        else:
            yield (lo, lo + r)

    def sort(lo, hi):
        if hi - lo >= 1:
            mid = lo + (hi - lo) // 2
            yield from sort(lo, mid)
            yield from sort(mid + 1, hi)
            yield from merge(lo, hi, 1)

    return list(sort(0, n - 1))


def _topk_cols(s, k):
    n, t = s.shape
    sub = 8
    assert n == sub * k
    row0 = lax.broadcasted_iota(jnp.int32, (sub, t), 0)
    v = [s[sub * g:sub * (g + 1)] for g in range(k)]
    r = [row0 + sub * g for g in range(k)]
    for a, b in _sort_network(k):
        first = (v[a] > v[b]) | ((v[a] == v[b]) & (r[a] < r[b]))
        v[a], v[b] = jnp.where(first, v[a], v[b]), jnp.where(first, v[b], v[a])
        r[a], r[b] = jnp.where(first, r[a], r[b]), jnp.where(first, r[b], r[a])
    vals, idxs = [], []
    for step in range(k):
        m = jnp.max(v[0], axis=0, keepdims=True)
        i = jnp.min(jnp.where(v[0] == m, r[0], _NO_POS), axis=0, keepdims=True)
        vals.append(m)
        idxs.append(i)
        col = r[0] == i
        for g in range(k - step - 1):
            v[g] = jnp.where(col, v[g + 1], v[g])
            r[g] = jnp.where(col, r[g + 1], r[g])
    return jnp.concatenate(vals, axis=0), jnp.concatenate(idxs, axis=0)


def _peer_route_kernel(q_ref, keys_ref, idx_ref, gate_ref, *, tt):
    K = PEER_TOPK
    half = PEER_QDIM // 2
    row_k = lax.broadcasted_iota(jnp.int32, (K, tt), 0)
    row_8 = lax.broadcasted_iota(jnp.int32, (8, tt), 0)
    assert K == 16 and all((i < 4) or (i < 8 and j < 4) or (j == 0)
                           for i in range(K) for j in range(K) if (i + 1) * (j + 1) <= K)
    idx_rows, gate_rows = [], []
    for h in range(PEER_HEADS):
        tops = []
        for p in range(2):
            qh = q_ref[:, (2 * h + p) * half:(2 * h + p + 1) * half]
            sc = _dot_nt_hi(keys_ref[h, p], qh)
            tops.append(_topk_cols(sc, K))
        (s0, i0), (s1, i1) = tops
        cb, pb, ib = [], [], []
        for i in range(4):
            cb.append(s0[i:i + 1] + s1)
            pb.append(i * K + row_k)
            ib.append(i0[i:i + 1] * PEER_NKEYS + i1)
        for j in range(4):
            cb.append(jnp.where(row_8 >= 4, s0[0:8] + s1[j:j + 1], -jnp.inf))
            pb.append(row_8 * K + j)
            ib.append(i0[0:8] * PEER_NKEYS + i1[j:j + 1])
        cb.append(s0[8:16] + s1[0:1])
        pb.append((row_8 + 8) * K)
        ib.append(i0[8:16] * PEER_NKEYS + i1[0:1])
        cand = jnp.concatenate(cb, axis=0)
        pos_c = jnp.concatenate(pb, axis=0)
        cidx = jnp.concatenate(ib, axis=0)
        best, bpos = _topk_rows(cand, K, pos_c)
        eidx = jnp.concatenate(
            [jnp.min(jnp.where(pos_c == bpos[r:r + 1], cidx, PEER_NKEYS * PEER_NKEYS),
                     axis=0, keepdims=True) for r in range(K)], axis=0)
        e = jnp.exp(best - best[0:1])
        g = e / jnp.sum(e, axis=0, keepdims=True)
        zero = jnp.zeros_like(g[0:1])
        for r in range(K):
            gate_rows += [zero, g[r:r + 1]]
        idx_rows.append(eidx)
    idx_ref[...] = jnp.concatenate(idx_rows, axis=0).T
    gate_ref[...] = jnp.concatenate(gate_rows, axis=0).T


def _peer_route(qry, sub_keys, tt=256):
    T, N = qry.shape
    PK = PEER_HEADS * PEER_TOPK
    return pl.pallas_call(
        functools.partial(_peer_route_kernel, tt=tt),
        grid=(T // tt,),
        in_specs=[pl.BlockSpec((tt, N), lambda i: (i, 0)),
                  pl.BlockSpec(sub_keys.shape, lambda i: (0, 0, 0, 0))],
        out_specs=[pl.BlockSpec((tt, PK), lambda i: (i, 0)), pl.BlockSpec((tt, 2 * PK), lambda i: (i, 0))],
        out_shape=[jax.ShapeDtypeStruct((T, PK), jnp.int32), jax.ShapeDtypeStruct((T, 2 * PK), F32)],
        compiler_params=_cparams(("parallel",)),
        name="peer_route",
    )(qry, sub_keys)


PEER_TOKENS_PER_STEP = 4
PEER_AHEAD = 2
PEER_TOKEN_GROUP = 16
PEER_BANKS = PEER_TOKEN_GROUP // PEER_TOKENS_PER_STEP
PEER_SLOTS = PEER_BANKS * PEER_TOKENS_PER_STEP


def _gelu(x):
    return 0.5 * x * (1.0 + lax.erf(x * (2.0 ** -0.5)))


def _pack_expert_table(u, v):
    def pack(x):
        E, D = x.shape
        b = lax.bitcast_convert_type(x.astype(BF16), jnp.uint16).astype(jnp.uint32)
        return (b[:, :D // 2] | (b[:, D // 2:] << 16)).reshape(E, D // (2 * LANES), LANES)
    return jnp.concatenate([pack(u), pack(v)], axis=1)


def _peer_expert_kernel(idx_ref, gate_ref, h_ref, y_ref, tab_ref, o_ref, buf, sem, *, tt, D):
    PK = PEER_HEADS * PEER_TOPK
    GT = PEER_TOKEN_GROUP
    HD = D // 2
    R = HD // LANES

    def issue(t, slot, k0=0, k1=PK):
        for k in range(k0, k1):
            e = idx_ref[t, k]
            pltpu.make_async_copy(tab_ref.at[e], buf.at[slot, :, k], sem.at[slot]).start(priority=k % 2)

    def wait(slot):
        pltpu.make_async_copy(buf.at[slot], buf.at[slot], sem.at[slot]).wait()

    TPS = PEER_TOKENS_PER_STEP

    AHEAD = PEER_AHEAD
    NSTEP = GT // TPS
    assert AHEAD < PEER_BANKS

    for t in range(AHEAD * TPS):
        issue(t, t)

    even = lax.broadcasted_iota(jnp.int32, (1, 2 * PK), 1) % 2 == 0

    def tokens(j0, bank, hr, gg, prefetch):
        zs = [jnp.zeros((1, 2 * PK), F32) for _ in range(TPS)]
        for r in range(R):
            for i in range(TPS):
                j = j0 + i
                ub = pltpu.bitcast(buf[bank + i, r], BF16)
                res = lax.dot_general(hr[r], ub, (((1,), (1,)), ((), ())), preferred_element_type=F32)
                zs[i] = zs[i] + jnp.where(even, res[j:j + 1], res[GT + j:GT + j + 1])
                prefetch(i, r)
        a2s = []
        for i in range(TPS):
            j = j0 + i
            s = zs[i] + pltpu.roll(zs[i], 1, 1)
            a = _gelu(s) * gg[j:j + 1]
            a2s.append(jnp.concatenate([pltpu.roll(a, 2 * PK - 1, 1), a], axis=0).astype(BF16))
        o2 = [[] for _ in range(TPS)]
        for r in range(R):
            for i in range(TPS):
                o2[i].append(jnp.dot(a2s[i], pltpu.bitcast(buf[bank + i, R + r], BF16),
                                     preferred_element_type=F32))
                prefetch(i, R + r)
        return [jnp.concatenate([o[0:1] for o in o2[i]] + [o[1:2] for o in o2[i]], axis=1)
                for i in range(TPS)]

    def group(t0, last):
        hg = h_ref[pl.ds(t0, GT), :]
        gg = gate_ref[pl.ds(t0, GT), :]
        hr = [jnp.concatenate([hg[:, r * LANES:(r + 1) * LANES],
                               hg[:, HD + r * LANES:HD + (r + 1) * LANES]], axis=0) for r in range(R)]
        outs = []
        for step in range(NSTEP):
            bank, nbank = step * TPS, ((step + AHEAD) % NSTEP) * TPS
            for i in range(TPS):
                wait(bank + i)

            def prefetch(i, stage, step=step, nbank=nbank):
                if not (last and step + AHEAD >= NSTEP):
                    per = PK // (2 * R)
                    issue(t0 + (step + AHEAD) * TPS + i, nbank + i, stage * per, (stage + 1) * per)
            outs += tokens(step * TPS, bank, hr, gg, prefetch)
        o_ref[pl.ds(t0, GT), :] = y_ref[pl.ds(t0, GT), :] + jnp.concatenate(outs, axis=0)

    def body(gi, c):
        group(pl.multiple_of(gi * GT, GT), False)
        return c

    lax.fori_loop(0, tt // GT - 1, body, 0)
    group(tt - GT, True)


def _peer_experts(idx, gate2, h2, y1, table, tt=256):
    T, D = y1.shape
    PK = PEER_HEADS * PEER_TOPK
    return pl.pallas_call(
        functools.partial(_peer_expert_kernel, tt=tt, D=D),
        grid=(T // tt,),
        in_specs=[
            pl.BlockSpec((tt, PK), lambda i: (i, 0), memory_space=pltpu.SMEM),
            pl.BlockSpec((tt, 2 * PK), lambda i: (i, 0)),
            pl.BlockSpec((tt, D), lambda i: (i, 0)),
            pl.BlockSpec((tt, D), lambda i: (i, 0)),
            pl.BlockSpec(memory_space=pl.ANY),
        ],
        out_specs=pl.BlockSpec((tt, D), lambda i: (i, 0)),
        out_shape=jax.ShapeDtypeStruct((T, D), F32),
        scratch_shapes=[pltpu.VMEM((PEER_SLOTS, table.shape[1], PK, LANES), jnp.uint32),
                        pltpu.SemaphoreType.DMA((PEER_SLOTS,))],
        compiler_params=pltpu.CompilerParams(dimension_semantics=("arbitrary",),
                                             vmem_limit_bytes=VMEM_LIMIT, disable_bounds_checks=True),
        name="peer_experts",
    )(idx, gate2, h2, y1, table)


def _layer(x, rel_bias, norm1_gain, w_in, conv_w, a_log, dt_bias, q_norm_gain, k_norm_gain,
           gdn_norm_gain, w_up_a, w_up_b, w_out, norm2_gain, peer_w_query, peer_sub_keys,
           peer_u, peer_v):
    B, S, D = x.shape
    T = B * S
    x2 = x.reshape(T, D)
    n_main = 3 * A_HEADS * A_HEAD_DIM + 4 * B_HEADS * B_KEY_DIM
    n_small = 2 * B_HEADS
    w_main = jnp.concatenate([w_in[:, :n_main], w_in[:, n_main + n_small:]], axis=1).astype(BF16)
    w_small = jnp.pad(w_in[:, n_main:n_main + n_small], ((0, 0), (0, LANES - n_small))).astype(BF16)
    proj, small = _in_proj(x2, norm1_gain[None, :], w_main, w_small)

    t0, t1, far = _bias_tiles(rel_bias)
    o_a = _moba(proj, q_norm_gain[None, :], k_norm_gain[None, :], t0, t1, far, B, S)

    pad8 = lambda v: jnp.pad(v, (B_HEADS, LANES - 2 * B_HEADS))[None, :]
    o_b = _gdn(proj, small, conv_w, pad8(a_log), pad8(dt_bias), gdn_norm_gain[None, :], B, S)

    merged = _merge(o_a, o_b, w_up_a.astype(BF16), w_up_b.astype(BF16), proj)
    y1, h2 = _out_proj(merged, w_out.astype(BF16), x2, norm2_gain[None, :])

    qry = _peer_query(h2, peer_w_query.astype(BF16))
    idx, gate2 = _peer_route(qry, peer_sub_keys)
    y = _peer_experts(idx, gate2, h2, y1, _pack_expert_table(peer_u, peer_v))
    return y.reshape(B, S, D)


def kernel(x, rel_bias, norm1_gain, w_in, conv_w, a_log, dt_bias, q_norm_gain, k_norm_gain,
           gdn_norm_gain, w_up_a, w_up_b, w_out, norm2_gain, peer_w_query, peer_sub_keys,
           peer_u, peer_v):
    depth = w_in.shape[0]
    for l in range(depth):
        x = _layer(x, rel_bias, norm1_gain[l], w_in[l], conv_w[l], a_log[l], dt_bias[l],
                   q_norm_gain[l], k_norm_gain[l], gdn_norm_gain[l], w_up_a[l], w_up_b[l],
                   w_out[l], norm2_gain[l], peer_w_query[l], peer_sub_keys[l], peer_u[l], peer_v[l])
    return x
```
